```python
import math
import jax, jax.numpy as jnp
from jax import lax
import numpy as np

D_MODEL = 1024
BATCH = 16
SEQ = 4096
DEPTH = 1
DEC_BATCH = 1
DEC_SEQ = 16384
PAST_LEN = 128

GRID_W = 64
HEAD_DIM = 64
NA_HEADS = 8
NA_WIDTH = NA_HEADS * HEAD_DIM
NA_ROWS_MAX = 8
NA_COLS = 16
NA_QCB = 16
NA_KCB = NA_QCB + NA_COLS
NA_NCB = GRID_W // NA_QCB
POOL_WINDOWS = (2, 4, 8, 16)
POOL_GROUP = 64
POOL_WIDTH = POOL_GROUP * len(POOL_WINDOWS)
MEM_HEADS = 4
MEM_WIDTH = MEM_HEADS * HEAD_DIM
N_MEM = 256
MIX_WIDTH = NA_WIDTH + POOL_WIDTH + MEM_WIDTH
IN_WIDTH = 3 * NA_WIDTH + POOL_WIDTH + MEM_WIDTH
PEER_HEADS = 8
PEER_NKEYS = 128
PEER_N = PEER_NKEYS * PEER_NKEYS
PEER_DKEY = 128
PEER_TOPK = 16
PEER_CHUNK = 128
ALPHA = (2.0 * DEPTH) ** 0.25
BETA = (8.0 * DEPTH) ** -0.25
LN_EPS = 1e-5

kernel_name = "hybrid_natten_pool_mem_peer_encoder"


def layer_norm(x, g, b):
    xf = x.astype(jnp.float32)
    mu = jnp.mean(xf, axis=-1, keepdims=True)
    var = jnp.mean(jnp.square(xf - mu), axis=-1, keepdims=True)
    y = (xf - mu) * lax.rsqrt(var + LN_EPS)
    return (y * g.astype(jnp.float32) + b.astype(jnp.float32)).astype(x.dtype)


def _na_static():
    j = np.arange(NA_NCB)
    kstart = np.clip(j * NA_QCB - NA_COLS // 2, 0, GRID_W - NA_KCB)
    kcol = kstart[:, None] + np.arange(NA_KCB)[None, :]
    qcol = j[:, None] * NA_QCB + np.arange(NA_QCB)[None, :]
    wstart = np.clip(qcol - NA_COLS // 2, 0, GRID_W - NA_COLS)
    kc = kcol[:, None, :]
    valid = (kc >= wstart[..., None]) & (kc < wstart[..., None] + NA_COLS)
    coff = np.clip(kc - qcol[..., None], -(NA_COLS - 1), NA_COLS - 1) + NA_COLS - 1
    return kcol, valid, coff


def neighborhood_attention(q, k, v, rpb):
    b, t, h, dh = q.shape
    rows = t // GRID_W
    kr = min(NA_ROWS_MAX, rows)
    kcol, valid, coff = _na_static()
    qg = q.reshape(b, rows, GRID_W, h, dh).transpose(1, 0, 2, 3, 4)
    qg = qg.reshape(rows, b, NA_NCB, NA_QCB, h, dh)
    kg = k.reshape(b, rows, GRID_W, h, dh)
    vg = v.reshape(b, rows, GRID_W, h, dh)
    col_bias = rpb[:, :, coff]
    scale = dh ** -0.5
    mask = valid[None, :, None, :, None, :]

    def row_block(args):
        r, q_r = args
        rs = jnp.clip(r - kr // 2, 0, rows - kr)
        k_band = lax.dynamic_slice_in_dim(kg, rs, kr, axis=1)[:, :, kcol]
        v_band = lax.dynamic_slice_in_dim(vg, rs, kr, axis=1)[:, :, kcol]
        roff = rs + jnp.arange(kr) - r + NA_ROWS_MAX - 1
        bias = col_bias[:, roff].transpose(2, 0, 3, 1, 4)
        s = jnp.einsum('bjqhd,bijkhd->bjhqik', q_r, k_band,
                       preferred_element_type=jnp.float32) * scale
        s = s + bias[None].astype(jnp.float32)
        s = jnp.where(mask, s, -jnp.inf)
        p = jax.nn.softmax(s, axis=(-2, -1))
        o = jnp.einsum('bjhqik,bijkhd->bjqhd', p.astype(v.dtype), v_band)
        return o.reshape(b, GRID_W, h * dh)

    out = lax.map(row_block, (jnp.arange(rows), qg))
    return out.transpose(1, 0, 2, 3).reshape(b, t, h * dh)


def multiscale_pool(xp, w_pool, pool_scale):
    b, t, _ = xp.shape
    xf = xp.astype(jnp.float32)
    csum = jnp.concatenate([jnp.zeros((b, 1, POOL_WIDTH), jnp.float32),
                            jnp.cumsum(xf, axis=1)], axis=1)
    pos = jnp.arange(t)
    outs = []
    for g, w in enumerate(POOL_WINDOWS):
        lo = jnp.clip(pos - w // 2, 0, t)
        hi = jnp.clip(pos + w // 2, 0, t)
        sl = slice(g * POOL_GROUP, (g + 1) * POOL_GROUP)
        c = csum[:, :, sl]
        mean = (c[:, hi] - c[:, lo]) / (hi - lo).astype(jnp.float32)[None, :, None]
        outs.append(mean - xf[:, :, sl])
    pooled = jnp.stack(outs, axis=2).astype(xp.dtype)
    y = jnp.einsum('btgc,gcd->btgd', pooled, w_pool).reshape(b, t, POOL_WIDTH)
    return y * pool_scale


def memory_attention(qm, mem, w_mem_kv):
    b, t, _ = qm.shape
    m = mem.shape[1]
    q = qm.reshape(b, t, MEM_HEADS, HEAD_DIM)
    kv = (mem @ w_mem_kv).reshape(b, m, 2, MEM_HEADS, HEAD_DIM)
    s = jnp.einsum('bthd,bmhd->bhtm', q, kv[:, :, 0],
                   preferred_element_type=jnp.float32) * (HEAD_DIM ** -0.5)
    p = jax.nn.softmax(s, axis=-1)
    o = jnp.einsum('bhtm,bmhd->bthd', p.astype(qm.dtype), kv[:, :, 1])
    return o.reshape(b, t, MEM_WIDTH)


def peer_ffn(x, w_query, sub_keys, peer_u, peer_v):
    b, t, d = x.shape
    xt = x.reshape(-1, PEER_CHUNK, d)

    def chunk(xc):
        q = (xc @ w_query).reshape(PEER_CHUNK, PEER_HEADS, 2, PEER_DKEY // 2)
        s = jnp.einsum('chpk,pnk->chpn', q, sub_keys,
                       preferred_element_type=jnp.float32)
        sv, si = lax.top_k(s, PEER_TOPK)
        cand = (sv[:, :, 0, :, None] + sv[:, :, 1, None, :]).reshape(PEER_CHUNK, PEER_HEADS, -1)
        cidx = (si[:, :, 0, :, None] * PEER_NKEYS + si[:, :, 1, None, :]).reshape(PEER_CHUNK, PEER_HEADS, -1)
        best, pos = lax.top_k(cand, PEER_TOPK)
        eidx = jnp.take_along_axis(cidx, pos, axis=-1).reshape(PEER_CHUNK, PEER_HEADS * PEER_TOPK)
        gate = jax.nn.softmax(best, axis=-1).reshape(PEER_CHUNK, PEER_HEADS * PEER_TOPK)
        u = jnp.take(peer_u, eidx, axis=0)
        act = jax.nn.gelu(jnp.einsum('cd,ced->ce', xc, u, preferred_element_type=jnp.float32),
                          approximate=False)
        wgt = (gate * act).astype(xc.dtype)
        vv = jnp.take(peer_v, eidx, axis=0)
        return jnp.einsum('ce,ced->cd', wgt, vv)

    return lax.map(chunk, xt).reshape(b, t, d)


def trunk(x, mem, ln_in_g, ln_in_b, w_in, w_mem_kv, na_rpb, w_pool, pool_scale, w_out,
          ln1_g, ln1_b, w_query, sub_keys, peer_u, peer_v, ln2_g, ln2_b):
    b, t, _ = x.shape
    x = layer_norm(x, ln_in_g, ln_in_b)
    splits = [NA_WIDTH, 2 * NA_WIDTH, 3 * NA_WIDTH, 3 * NA_WIDTH + POOL_WIDTH]
    for l in range(DEPTH):
        p = x @ w_in[l]
        q_na, k_na, v_na, x_pool, q_mem = jnp.split(p, splits, axis=-1)
        hs = (b, t, NA_HEADS, HEAD_DIM)
        y_na = neighborhood_attention(q_na.reshape(hs), k_na.reshape(hs), v_na.reshape(hs), na_rpb[l])
        y_pool = multiscale_pool(x_pool, w_pool[l], pool_scale[l])
        y_mem = memory_attention(q_mem, mem, w_mem_kv[l])
        mixed = jnp.concatenate([y_na, y_pool, y_mem], axis=-1)
        x = layer_norm(ALPHA * x + mixed @ w_out[l], ln1_g[l], ln1_b[l])
        x = layer_norm(ALPHA * x + peer_ffn(x, w_query[l], sub_keys[l], peer_u[l], peer_v[l]),
                       ln2_g[l], ln2_b[l])
    return x


def setup_inputs(seed: int = 0) -> dict:
    key = jax.random.key(seed)
    ks = jax.random.split(key, 24)
    f32 = jnp.float32
    nrm = lambda k, shape, s: jax.random.normal(k, shape, f32) * s
    return {
        "x_prompt": nrm(ks[0], (BATCH, SEQ, D_MODEL), 1.0),
        "x_sample": nrm(ks[1], (DEC_BATCH, DEC_SEQ, D_MODEL), 1.0),
        "mem_prompt": nrm(ks[2], (BATCH, N_MEM, D_MODEL), 1.0),
        "mem_sample": nrm(ks[3], (DEC_BATCH, N_MEM, D_MODEL), 1.0),
        "ln_in_g": 1.0 + nrm(ks[4], (D_MODEL,), 0.02),
        "ln_in_b": nrm(ks[5], (D_MODEL,), 0.02),
        "w_in": nrm(ks[6], (DEPTH, D_MODEL, IN_WIDTH), D_MODEL ** -0.5),
        "w_mem_kv": nrm(ks[7], (DEPTH, D_MODEL, 2 * MEM_WIDTH), D_MODEL ** -0.5),
        "na_rpb": nrm(ks[8], (DEPTH, NA_HEADS, 2 * NA_ROWS_MAX - 1, 2 * NA_COLS - 1), 0.1),
        "w_pool": nrm(ks[9], (DEPTH, len(POOL_WINDOWS), POOL_GROUP, POOL_GROUP), POOL_GROUP ** -0.5),
        "pool_scale": 1.0 + nrm(ks[10], (DEPTH, POOL_WIDTH), 0.02),
        "w_out": nrm(ks[11], (DEPTH, MIX_WIDTH, D_MODEL), BETA * MIX_WIDTH ** -0.5),
        "ln1_g": 1.0 + nrm(ks[12], (DEPTH, D_MODEL), 0.02),
        "ln1_b": nrm(ks[13], (DEPTH, D_MODEL), 0.02),
        "w_query": nrm(ks[14], (DEPTH, D_MODEL, PEER_HEADS * PEER_DKEY), D_MODEL ** -0.5),
        "sub_keys": nrm(ks[15], (DEPTH, 2, PEER_NKEYS, PEER_DKEY // 2), (PEER_DKEY // 2) ** -0.5),
        "peer_u": nrm(ks[16], (DEPTH, PEER_N, D_MODEL), D_MODEL ** -0.5),
        "peer_v": nrm(ks[17], (DEPTH, PEER_N, D_MODEL), BETA),
        "ln2_g": 1.0 + nrm(ks[18], (DEPTH, D_MODEL), 0.02),
        "ln2_b": nrm(ks[19], (DEPTH, D_MODEL), 0.02),
    }


def reference(x_prompt, x_sample, mem_prompt, mem_sample, ln_in_g, ln_in_b, w_in, w_mem_kv,
              na_rpb, w_pool, pool_scale, w_out, ln1_g, ln1_b, w_query, sub_keys, peer_u,
              peer_v, ln2_g, ln2_b):
    y_prompt = trunk(x_prompt, mem_prompt, ln_in_g, ln_in_b, w_in, w_mem_kv, na_rpb, w_pool,
                     pool_scale, w_out, ln1_g, ln1_b, w_query, sub_keys, peer_u, peer_v,
                     ln2_g, ln2_b)
    y_sample = trunk(x_sample, mem_sample, ln_in_g, ln_in_b, w_in, w_mem_kv, na_rpb, w_pool,
                     pool_scale, w_out, ln1_g, ln1_b, w_query, sub_keys, peer_u, peer_v,
                     ln2_g, ln2_b)
    return (y_prompt, y_sample)
```

```python
import functools

import numpy as np
import jax
import jax.numpy as jnp
from jax import lax
from jax.experimental import pallas as pl
from jax.experimental.pallas import tpu as pltpu

F32 = jnp.float32
BF16 = jnp.bfloat16

D_MODEL = 1024
GRID_W = 64
HEAD_DIM = 64
NA_HEADS = 8
NA_WIDTH = NA_HEADS * HEAD_DIM
NA_ROWS = 8
NA_COLS = 16
POOL_WINDOWS = (2, 4, 8, 16)
POOL_GROUP = 64
POOL_WIDTH = POOL_GROUP * len(POOL_WINDOWS)
MEM_HEADS = 4
MEM_WIDTH = MEM_HEADS * HEAD_DIM
PEER_HEADS = 8
PEER_NKEYS = 128
PEER_N = PEER_NKEYS * PEER_NKEYS
PEER_HALF = 64
PEER_TOPK = 16
DEPTH = 1
ALPHA = (2.0 * DEPTH) ** 0.25
LN_EPS = 1e-5
QK_SCALE = HEAD_DIM ** -0.5
NEG = -1e30
INV_SQRT2 = 0.7071067811865476

LANES = 128
SUPER_ROW = 2 * GRID_W
NA_WIN = 5
NA_BIAS_SPAN = 9
VMEM_LIMIT = 56 * 1024 * 1024

TOKEN_TILE = 512
ROUTE_TILE = 256
PEER_TILE = 512
PEER_CHUNK = 1024


def _cparams(sem):
    return pltpu.CompilerParams(dimension_semantics=sem, vmem_limit_bytes=VMEM_LIMIT)


def _layer_norm(x, g, b):
    mu = jnp.mean(x, axis=-1, keepdims=True)
    xc = x - mu
    var = jnp.mean(xc * xc, axis=-1, keepdims=True)
    return xc * lax.rsqrt(var + LN_EPS) * g + b


def _dot(a, b):
    return jnp.dot(a, b, preferred_element_type=F32)


def _dot_nt(a, b):
    return lax.dot_general(a, b, (((1,), (1,)), ((), ())), preferred_element_type=F32)


def _in_proj_kernel(x_ref, g_ref, b_ref, wql_ref, wqh_ref, wkt_ref, wv_ref, wp_ref, wml_ref, wmh_ref,
                    xn_ref, ql_ref, qh_ref, kt_ref, v_ref, xp_ref, ml_ref, mh_ref):
    y = _layer_norm(x_ref[...], g_ref[...], b_ref[...])
    xn_ref[...] = y
    yb = y.astype(BF16)
    ql_ref[...] = (_dot(yb, wql_ref[...]) * QK_SCALE).astype(BF16)
    qh_ref[...] = (_dot(yb, wqh_ref[...]) * QK_SCALE).astype(BF16)
    kt_ref[...] = _dot_nt(wkt_ref[...], yb).astype(BF16)
    v_ref[...] = _dot(yb, wv_ref[...]).astype(BF16)
    xp_ref[...] = _dot(yb, wp_ref[...])
    ml_ref[...] = (_dot(yb, wml_ref[...]) * QK_SCALE).astype(BF16)
    mh_ref[...] = (_dot(yb, wmh_ref[...]) * QK_SCALE).astype(BF16)


def _in_proj(x, g, b, wql, wqh, wkt, wv, wp, wml, wmh):
    n = x.shape[0]
    tm = TOKEN_TILE
    row = lambda w: pl.BlockSpec((tm, w), lambda i: (i, 0))
    full = lambda a: pl.BlockSpec(a.shape, lambda i: (0,) * a.ndim)
    return pl.pallas_call(
        _in_proj_kernel,
        grid=(n // tm,),
        in_specs=[row(D_MODEL), full(g), full(b), full(wql), full(wqh), full(wkt), full(wv), full(wp),
                  full(wml), full(wmh)],
        out_specs=[row(D_MODEL), row(NA_WIDTH), row(NA_WIDTH),
                   pl.BlockSpec((NA_WIDTH, tm), lambda i: (0, i)),
                   row(NA_WIDTH), row(POOL_WIDTH), row(MEM_WIDTH), row(MEM_WIDTH)],
        out_shape=[jax.ShapeDtypeStruct((n, D_MODEL), F32),
                   jax.ShapeDtypeStruct((n, NA_WIDTH), BF16),
                   jax.ShapeDtypeStruct((n, NA_WIDTH), BF16),
                   jax.ShapeDtypeStruct((NA_WIDTH, n), BF16),
                   jax.ShapeDtypeStruct((n, NA_WIDTH), BF16),
                   jax.ShapeDtypeStruct((n, POOL_WIDTH), F32),
                   jax.ShapeDtypeStruct((n, MEM_WIDTH), BF16),
                   jax.ShapeDtypeStruct((n, MEM_WIDTH), BF16)],
        compiler_params=_cparams(("parallel",)),
        name="in_proj",
    )(x, g, b, wql, wqh, wkt, wv, wp, wml, wmh)


def _pair_attention(q_lo, q_hi, kt, v, bias_lo, bias_hi):
    outs = []
    for q, bias in ((q_lo, bias_lo), (q_hi, bias_hi)):
        s = _dot(q, kt) + bias
        mx = jnp.max(s, axis=-1, keepdims=True)
        p = jnp.exp(s - mx)
        l = jnp.sum(p, axis=-1, keepdims=True)
        outs.append(_dot(p.astype(BF16), v) * (1.0 / l))
    lane = lax.broadcasted_iota(jnp.int32, outs[0].shape, 1)
    return jnp.where(lane < HEAD_DIM, outs[0], outs[1])


def _na_kernel(ql_ref, qh_ref, *refs, n_sr, rows):
    kt_refs = refs[0:NA_WIN]
    v_refs = refs[NA_WIN:2 * NA_WIN]
    bias_ref = refs[2 * NA_WIN]
    o_ref = refs[2 * NA_WIN + 1]
    kwin, vwin = refs[2 * NA_WIN + 2:]
    r = pl.program_id(1)
    sr0 = jnp.clip(r - 2, 0, n_sr - NA_WIN)
    for s in range(NA_WIN):
        kwin[:, s * SUPER_ROW:(s + 1) * SUPER_ROW] = kt_refs[s][...]
        vwin[s * SUPER_ROW:(s + 1) * SUPER_ROW, :] = v_refs[s][...]
    nk = NA_WIN * SUPER_ROW
    qi = lax.broadcasted_iota(jnp.int32, (SUPER_ROW, nk), 0)
    ki = lax.broadcasted_iota(jnp.int32, (SUPER_ROW, nk), 1)
    qrow = 2 * r + jnp.where(qi >= GRID_W, 1, 0)
    krow = 2 * sr0 + lax.shift_right_logical(ki, 6)
    rs = jnp.clip(qrow - NA_ROWS // 2, 0, rows - NA_ROWS)
    rmask = jnp.where((krow >= rs) & (krow < rs + NA_ROWS), 0.0, NEG).astype(F32)
    boff = pl.multiple_of((sr0 - r + NA_BIAS_SPAN // 2) * SUPER_ROW, SUPER_ROW)
    for m in range(NA_HEADS // 2):
        cols = slice(m * LANES, (m + 1) * LANES)
        b_lo = bias_ref[2 * m, :, pl.ds(boff, nk)] + rmask
        b_hi = bias_ref[2 * m + 1, :, pl.ds(boff, nk)] + rmask
        o = _pair_attention(ql_ref[:, cols], qh_ref[:, cols], kwin[cols, :], vwin[:, cols], b_lo, b_hi)
        o_ref[:, cols] = o.astype(BF16)


def _na_attention(ql, qh, kt, v, bias, batch, rows):
    n = ql.shape[0]
    n_sr = rows // 2
    assert rows >= 2 * NA_WIN and rows % 2 == 0 and n == batch * n_sr * SUPER_ROW
    qspec = pl.BlockSpec((SUPER_ROW, NA_WIDTH), lambda b, r: (b * n_sr + r, 0))

    def win(r):
        return jnp.clip(r - 2, 0, n_sr - NA_WIN)

    kt_specs = [pl.BlockSpec((NA_WIDTH, SUPER_ROW), lambda b, r, s=s: (0, b * n_sr + win(r) + s))
                for s in range(NA_WIN)]
    v_specs = [pl.BlockSpec((SUPER_ROW, NA_WIDTH), lambda b, r, s=s: (b * n_sr + win(r) + s, 0))
               for s in range(NA_WIN)]
    return pl.pallas_call(
        functools.partial(_na_kernel, n_sr=n_sr, rows=rows),
        grid=(batch, n_sr),
        in_specs=[qspec, qspec] + kt_specs + v_specs + [pl.BlockSpec(bias.shape, lambda b, r: (0, 0, 0))],
        out_specs=qspec,
        out_shape=jax.ShapeDtypeStruct((n, NA_WIDTH), BF16),
        scratch_shapes=[pltpu.VMEM((NA_WIDTH, NA_WIN * SUPER_ROW), BF16),
                        pltpu.VMEM((NA_WIN * SUPER_ROW, NA_WIDTH), BF16)],
        compiler_params=_cparams(("parallel", "parallel")),
        name="na_attn",
    )(ql, qh, *([kt] * NA_WIN), *([v] * NA_WIN), bias)


def _na_bias_table(rpb):
    qp = np.arange(SUPER_ROW) // GRID_W
    qc = np.arange(SUPER_ROW) % GRID_W
    slot = np.arange(NA_BIAS_SPAN * SUPER_ROW)
    dsr = slot // SUPER_ROW - NA_BIAS_SPAN // 2
    kp = (slot % SUPER_ROW) // GRID_W
    kc = slot % GRID_W
    rel = 2 * dsr[None, :] + kp[None, :] - qp[:, None]
    row_ok = np.abs(rel) <= NA_ROWS - 1
    ridx = np.clip(rel + NA_ROWS - 1, 0, 2 * NA_ROWS - 2)
    wstart = np.clip(qc - NA_COLS // 2, 0, GRID_W - NA_COLS)
    col_ok = (kc[None, :] >= wstart[:, None]) & (kc[None, :] < wstart[:, None] + NA_COLS)
    cidx = np.clip(kc[None, :] - qc[:, None], -(NA_COLS - 1), NA_COLS - 1) + NA_COLS - 1
    tab = rpb[:, ridx, cidx].astype(F32)
    tab = jnp.where(row_ok[None], tab, 0.0)
    return jnp.where(col_ok[None], tab, NEG)


def _pool_kernel(prev_ref, cur_ref, next_ref, w_ref, sc_ref, o_ref, buf, *, seq, tm):
    i = pl.program_id(1)
    nt = seq // tm
    halo = 8
    zero = jnp.zeros((halo, POOL_WIDTH), F32)
    buf[0:halo, :] = jnp.where(i == 0, zero, prev_ref[...])
    buf[halo:halo + tm, :] = cur_ref[...]
    buf[halo + tm:2 * halo + tm, :] = jnp.where(i == nt - 1, zero, next_ref[...])
    pos = i * tm + lax.broadcasted_iota(jnp.int32, (tm, LANES), 0)
    lane = lax.broadcasted_iota(jnp.int32, (tm, LANES), 1)

    def shifted(d, c):
        return buf[halo + d:halo + d + tm, c * LANES:(c + 1) * LANES]

    def count(w):
        return (jnp.minimum(pos + w // 2, seq) - jnp.maximum(pos - w // 2, 0)).astype(F32)

    x0 = shifted(0, 0)
    s2 = shifted(-1, 0) + x0
    s4 = s2 + shifted(-2, 0) + shifted(1, 0)
    pa = jnp.where(lane < POOL_GROUP, s2 / count(2), s4 / count(4)) - x0
    x1 = shifted(0, 1)
    s8 = x1
    for d in (-4, -3, -2, -1, 1, 2, 3):
        s8 = s8 + shifted(d, 1)
    s16 = s8
    for d in (-8, -7, -6, -5, 4, 5, 6, 7):
        s16 = s16 + shifted(d, 1)
    pb = jnp.where(lane < POOL_GROUP, s8 / count(8), s16 / count(16)) - x1
    pooled = jnp.concatenate([pa, pb], axis=1).astype(BF16)
    o_ref[...] = (_dot(pooled, w_ref[...]) * sc_ref[...]).astype(BF16)


def _pool(xp, w_bd, scale, batch, seq):
    n = xp.shape[0]
    tm = TOKEN_TILE
    nt = seq // tm
    assert seq % tm == 0 and n == batch * seq
    return pl.pallas_call(
        functools.partial(_pool_kernel, seq=seq, tm=tm),
        grid=(batch, nt),
        in_specs=[
            pl.BlockSpec((8, POOL_WIDTH), lambda b, i: (jnp.maximum((b * seq + i * tm) // 8 - 1, 0), 0)),
            pl.BlockSpec((tm, POOL_WIDTH), lambda b, i: (b * nt + i, 0)),
            pl.BlockSpec((8, POOL_WIDTH),
                         lambda b, i: (jnp.minimum((b * seq + (i + 1) * tm) // 8, n // 8 - 1), 0)),
            pl.BlockSpec(w_bd.shape, lambda b, i: (0, 0)),
            pl.BlockSpec(scale.shape, lambda b, i: (0, 0)),
        ],
        out_specs=pl.BlockSpec((tm, POOL_WIDTH), lambda b, i: (b * nt + i, 0)),
        out_shape=jax.ShapeDtypeStruct((n, POOL_WIDTH), BF16),
        scratch_shapes=[pltpu.VMEM((tm + 16, POOL_WIDTH), F32)],
        compiler_params=_cparams(("parallel", "parallel")),
        name="pool",
    )(xp, xp, xp, w_bd, scale)


def _mem_kv_kernel(mem_ref, wkt_ref, wv_ref, kt_ref, v_ref):
    mb = mem_ref[0].astype(BF16)
    kt_ref[0] = _dot_nt(wkt_ref[...], mb).astype(BF16)
    v_ref[0] = _dot(mb, wv_ref[...]).astype(BF16)


def _mem_kv(mem, wkt, wv):
    b, m, _ = mem.shape
    return pl.pallas_call(
        _mem_kv_kernel,
        grid=(b,),
        in_specs=[pl.BlockSpec((1, m, D_MODEL), lambda i: (i, 0, 0)),
                  pl.BlockSpec(wkt.shape, lambda i: (0, 0)),
                  pl.BlockSpec(wv.shape, lambda i: (0, 0))],
        out_specs=[pl.BlockSpec((1, MEM_WIDTH, m), lambda i: (i, 0, 0)),
                   pl.BlockSpec((1, m, MEM_WIDTH), lambda i: (i, 0, 0))],
        out_shape=[jax.ShapeDtypeStruct((b, MEM_WIDTH, m), BF16),
                   jax.ShapeDtypeStruct((b, m, MEM_WIDTH), BF16)],
        compiler_params=_cparams(("parallel",)),
        name="mem_kv",
    )(mem, wkt, wv)


def _mem_attn_kernel(ql_ref, qh_ref, kt_ref, v_ref, o_ref):
    zero = jnp.zeros((1, 1), F32)
    for m in range(MEM_HEADS // 2):
        cols = slice(m * LANES, (m + 1) * LANES)
        o = _pair_attention(ql_ref[:, cols], qh_ref[:, cols], kt_ref[0, cols, :], v_ref[0, :, cols],
                            zero, zero)
        o_ref[:, cols] = o.astype(BF16)


def _mem_attention(ql, qh, kt, v, batch, seq):
    n = ql.shape[0]
    tm = TOKEN_TILE
    nt = seq // tm
    m = kt.shape[2]
    qspec = pl.BlockSpec((tm, MEM_WIDTH), lambda b, i: (b * nt + i, 0))
    return pl.pallas_call(
        _mem_attn_kernel,
        grid=(batch, nt),
        in_specs=[qspec, qspec,
                  pl.BlockSpec((1, MEM_WIDTH, m), lambda b, i: (b, 0, 0)),
                  pl.BlockSpec((1, m, MEM_WIDTH), lambda b, i: (b, 0, 0))],
        out_specs=qspec,
        out_shape=jax.ShapeDtypeStruct((n, MEM_WIDTH), BF16),
        compiler_params=_cparams(("parallel", "parallel")),
        name="mem_attn",
    )(ql, qh, kt, v)


def _out_proj_kernel(xn_ref, na_ref, pool_ref, mem_ref, w1_ref, w2_ref, w3_ref, g_ref, b_ref, o_ref):
    mixed = _dot(na_ref[...], w1_ref[...]) + _dot(pool_ref[...], w2_ref[...]) + _dot(mem_ref[...], w3_ref[...])
    o_ref[...] = _layer_norm(ALPHA * xn_ref[...] + mixed, g_ref[...], b_ref[...])


def _out_proj(xn, y_na, y_pool, y_mem, w1, w2, w3, g, b):
    n = xn.shape[0]
    tm = TOKEN_TILE
    row = lambda w: pl.BlockSpec((tm, w), lambda i: (i, 0))
    full = lambda a: pl.BlockSpec(a.shape, lambda i: (0,) * a.ndim)
    return pl.pallas_call(
        _out_proj_kernel,
        grid=(n // tm,),
        in_specs=[row(D_MODEL), row(NA_WIDTH), row(POOL_WIDTH), row(MEM_WIDTH),
                  full(w1), full(w2), full(w3), full(g), full(b)],
        out_specs=row(D_MODEL),
        out_shape=jax.ShapeDtypeStruct((n, D_MODEL), F32),
        compiler_params=_cparams(("parallel",)),
        name="out_proj",
    )(xn, y_na, y_pool, y_mem, w1, w2, w3, g, b)


def _staircase():
    return [(a, b) for a in range(PEER_TOPK) for b in range(PEER_TOPK) if (a + 1) * (b + 1) <= PEER_TOPK]


def _route_kernel(x_ref, wqt_ref, keys_ref, rank_ref, e1_ref, thr_ref, a_ref, sc, sv):
    xb = x_ref[...].astype(BF16)
    qt = _dot_nt(wqt_ref[...], xb).astype(BF16)
    for hp in range(2 * PEER_HEADS):
        sc[hp] = _dot(keys_ref[hp % 2], qt[hp * PEER_HALF:(hp + 1) * PEER_HALF, :])

    ninf = F32(-jnp.inf)
    for h in range(PEER_HEADS):
        for p in range(2):
            s = sc[2 * h + p]
            rank = jnp.zeros_like(s)
            lt = None
            for k in range(PEER_TOPK):
                cand = s if lt is None else jnp.where(lt, s, ninf)
                m = jnp.max(cand, axis=0, keepdims=True)
                sv[p, k, h:h + 1, :] = m
                lt = s < m
                if p == 1:
                    rank = rank + jnp.where(lt, 1.0, 0.0)
            if p == 1:
                rank_ref[h] = rank

    sv0 = [sv[0, k] for k in range(PEER_TOPK)]
    sv1 = [sv[1, k] for k in range(PEER_TOPK)]
    cands = [sv0[a] + sv1[b] for a, b in _staircase()]
    top = cands[0]
    tau = top
    for _ in range(PEER_TOPK - 1):
        nxt = None
        for c in cands[1:]:
            v = jnp.where(c < tau, c, ninf)
            nxt = v if nxt is None else jnp.maximum(nxt, v)
        tau = nxt
    z = None
    for c in cands:
        v = jnp.where(c >= tau, jnp.exp(c - top), 0.0)
        z = v if z is None else z + v
    inv_z = 1.0 / z

    for h in range(PEER_HEADS):
        hs = slice(h, h + 1)
        s0 = sc[2 * h]
        s1 = sc[2 * h + 1]
        t_h = tau[hs]
        thr = jnp.zeros_like(s0)
        for b in range(PEER_TOPK):
            thr = thr + jnp.where(s0 + sv1[b][hs] >= t_h, 1.0, 0.0)
        thr_ref[h] = thr
        a_ref[h] = jnp.exp(s0 - sv0[0][hs]) * inv_z[hs]
        e1_ref[h] = jnp.exp(s1 - sv1[0][hs])


def _route(x1, wqt, keys):
    n = x1.shape[0]
    tr = ROUTE_TILE
    ospec = pl.BlockSpec((PEER_HEADS, PEER_NKEYS, tr), lambda i: (0, 0, i))
    oshape = jax.ShapeDtypeStruct((PEER_HEADS, PEER_NKEYS, n), F32)
    return pl.pallas_call(
        _route_kernel,
        grid=(n // tr,),
        in_specs=[pl.BlockSpec((tr, D_MODEL), lambda i: (i, 0)),
                  pl.BlockSpec(wqt.shape, lambda i: (0, 0)),
                  pl.BlockSpec(keys.shape, lambda i: (0, 0, 0))],
        out_specs=[ospec] * 4,
        out_shape=[oshape] * 4,
        scratch_shapes=[pltpu.VMEM((2 * PEER_HEADS, PEER_NKEYS, tr), F32),
                        pltpu.VMEM((2, PEER_TOPK, PEER_HEADS, tr), F32)],
        compiler_params=_cparams(("parallel",)),
        name="peer_route",
    )(x1, wqt, keys)


_SUB_ROWS = 32
_BLOCKS = PEER_CHUNK // PEER_NKEYS


def _peer_kernel(x_ref, u_ref, vt_ref, rank_ref, e1_ref, thr_ref, a_ref, g_ref, b_ref, o_ref,
                 xt, ht, wt, acc):
    j = pl.program_id(1)
    tm = x_ref.shape[0]

    @pl.when(j == 0)
    def _():
        xt[...] = x_ref[...].T.astype(BF16)
        acc[...] = jnp.zeros_like(acc)

    ht[...] = _dot(u_ref[...], xt[...])

    n_rg = PEER_NKEYS // _SUB_ROWS

    def tile_body(idx, carry):
        c0 = pl.multiple_of((idx // n_rg) * LANES, LANES)
        r0 = pl.multiple_of((idx % n_rg) * _SUB_ROWS, _SUB_ROWS)
        gates = [jnp.zeros((_SUB_ROWS, LANES), F32) for _ in range(_BLOCKS)]
        for h in range(PEER_HEADS):
            rk = rank_ref[h, pl.ds(r0, _SUB_ROWS), pl.ds(c0, LANES)]
            e1 = e1_ref[h, pl.ds(r0, _SUB_ROWS), pl.ds(c0, LANES)]
            for blk in range(_BLOCKS):
                thr = thr_ref[h, blk:blk + 1, pl.ds(c0, LANES)]
                a = a_ref[h, blk:blk + 1, pl.ds(c0, LANES)]
                gates[blk] = gates[blk] + jnp.where(rk < thr, e1, 0.0) * a
        for blk in range(_BLOCKS):
            rows = pl.ds(pl.multiple_of(blk * PEER_NKEYS + r0, _SUB_ROWS), _SUB_ROWS)
            hh = ht[rows, pl.ds(c0, LANES)]
            act = 0.5 * hh * (1.0 + lax.erf(hh * INV_SQRT2))
            wt[rows, pl.ds(c0, LANES)] = (gates[blk] * act).astype(BF16)
        return carry

    lax.fori_loop(0, (tm // LANES) * n_rg, tile_body, 0)
    acc[...] += _dot(vt_ref[...], wt[...])

    @pl.when(j == pl.num_programs(1) - 1)
    def _():
        o_ref[...] = _layer_norm(ALPHA * x_ref[...] + acc[...].T, g_ref[...], b_ref[...])


def _peer(x1, u, vt, rank, e1, thr, a, g, b):
    n = x1.shape[0]
    tm = PEER_TILE
    ec = PEER_CHUNK
    dense = pl.BlockSpec((PEER_HEADS, PEER_NKEYS, tm), lambda i, j: (0, 0, i))
    per_blk = pl.BlockSpec((PEER_HEADS, _BLOCKS, tm), lambda i, j: (0, j, i))
    vec = pl.BlockSpec((1, D_MODEL), lambda i, j: (0, 0))
    return pl.pallas_call(
        _peer_kernel,
        grid=(n // tm, PEER_N // ec),
        in_specs=[pl.BlockSpec((tm, D_MODEL), lambda i, j: (i, 0)),
                  pl.BlockSpec((ec, D_MODEL), lambda i, j: (j, 0)),
                  pl.BlockSpec((D_MODEL, ec), lambda i, j: (0, j)),
                  dense, dense, per_blk, per_blk, vec, vec],
        out_specs=pl.BlockSpec((tm, D_MODEL), lambda i, j: (i, 0)),
        out_shape=jax.ShapeDtypeStruct((n, D_MODEL), F32),
        scratch_shapes=[pltpu.VMEM((D_MODEL, tm), BF16),
                        pltpu.VMEM((ec, tm), F32),
                        pltpu.VMEM((ec, tm), BF16),
                        pltpu.VMEM((D_MODEL, tm), F32)],
        compiler_params=_cparams(("parallel", "arbitrary")),
        name="peer_dense",
    )(x1, u, vt, rank, e1, thr, a, g, b)


def _head_pair_masks(width):
    head = np.arange(width) // HEAD_DIM
    lo = (head % 2 == 0).astype(np.float32)
    return lo, 1.0 - lo


def _prepare(ln_in_g, ln_in_b, w_in, w_mem_kv, na_rpb, w_pool, pool_scale, w_out, ln1_g, ln1_b,
             w_query, sub_keys, peer_u, peer_v, ln2_g, ln2_b):
    wi = w_in[0]
    o = NA_WIDTH
    wq, wk, wv = wi[:, 0:o], wi[:, o:2 * o], wi[:, 2 * o:3 * o]
    wp = wi[:, 3 * o:3 * o + POOL_WIDTH]
    wm = wi[:, 3 * o + POOL_WIDTH:]
    lo, hi = _head_pair_masks(NA_WIDTH)
    mlo, mhi = _head_pair_masks(MEM_WIDTH)
    w_bd = jnp.zeros((POOL_WIDTH, POOL_WIDTH), F32)
    for g in range(len(POOL_WINDOWS)):
        sl = slice(g * POOL_GROUP, (g + 1) * POOL_GROUP)
        w_bd = w_bd.at[sl, sl].set(w_pool[0, g])
    wkv = w_mem_kv[0]
    wo = w_out[0].astype(BF16)
    row = lambda v: v.reshape(1, -1).astype(F32)
    return dict(
        ln_in_g=row(ln_in_g), ln_in_b=row(ln_in_b),
        wql=(wq * lo).astype(BF16), wqh=(wq * hi).astype(BF16), wkt=wk.T.astype(BF16), wv=wv.astype(BF16),
        wp=wp.astype(BF16), wml=(wm * mlo).astype(BF16), wmh=(wm * mhi).astype(BF16),
        mem_wkt=wkv[:, :MEM_WIDTH].T.astype(BF16), mem_wv=wkv[:, MEM_WIDTH:].astype(BF16),
        na_bias=_na_bias_table(na_rpb[0]),
        w_bd=w_bd.astype(BF16), pool_scale=row(pool_scale[0]),
        wo_na=wo[:NA_WIDTH], wo_pool=wo[NA_WIDTH:NA_WIDTH + POOL_WIDTH], wo_mem=wo[NA_WIDTH + POOL_WIDTH:],
        ln1_g=row(ln1_g[0]), ln1_b=row(ln1_b[0]),
        wqt=w_query[0].T.astype(BF16), keys=sub_keys[0].astype(BF16),
        u=peer_u[0].astype(BF16), vt=peer_v[0].T.astype(BF16),
        ln2_g=row(ln2_g[0]), ln2_b=row(ln2_b[0]),
    )


def _trunk(x, mem, w):
    batch, seq, _ = x.shape
    rows = seq // GRID_W
    xf = x.reshape(batch * seq, D_MODEL)
    xn, ql, qh, kt, v, xp, ml, mh = _in_proj(xf, w["ln_in_g"], w["ln_in_b"], w["wql"], w["wqh"], w["wkt"],
                                             w["wv"], w["wp"], w["wml"], w["wmh"])
    y_na = _na_attention(ql, qh, kt, v, w["na_bias"], batch, rows)
    y_pool = _pool(xp, w["w_bd"], w["pool_scale"], batch, seq)
    mkt, mv = _mem_kv(mem, w["mem_wkt"], w["mem_wv"])
    y_mem = _mem_attention(ml, mh, mkt, mv, batch, seq)
    x1 = _out_proj(xn, y_na, y_pool, y_mem, w["wo_na"], w["wo_pool"], w["wo_mem"], w["ln1_g"], w["ln1_b"])
    rank, e1, thr, a = _route(x1, w["wqt"], w["keys"])
    y = _peer(x1, w["u"], w["vt"], rank, e1, thr, a, w["ln2_g"], w["ln2_b"])
    return y.reshape(batch, seq, D_MODEL)


def kernel(x_prompt, x_sample, mem_prompt, mem_sample, ln_in_g, ln_in_b, w_in, w_mem_kv, na_rpb, w_pool, pool_scale, w_out, ln1_g, ln1_b, w_query, sub_keys, peer_u, peer_v, ln2_g, ln2_b):
    w = _prepare(ln_in_g, ln_in_b, w_in, w_mem_kv, na_rpb, w_pool, pool_scale, w_out, ln1_g, ln1_b,
                 w_query, sub_keys, peer_u, peer_v, ln2_g, ln2_b)
    return (_trunk(x_prompt, mem_prompt, w), _trunk(x_sample, mem_sample, w))
```

```python
import functools

import numpy as np
import jax
import jax.numpy as jnp
from jax import lax
from jax.experimental import pallas as pl
from jax.experimental.pallas import tpu as pltpu

F32 = jnp.float32
BF16 = jnp.bfloat16

D_MODEL = 1024
GRID_W = 64
HEAD_DIM = 64
NA_HEADS = 8
NA_WIDTH = NA_HEADS * HEAD_DIM
NA_ROWS = 8
NA_COLS = 16
POOL_WINDOWS = (2, 4, 8, 16)
POOL_GROUP = 64
POOL_WIDTH = POOL_GROUP * len(POOL_WINDOWS)
MEM_HEADS = 4
MEM_WIDTH = MEM_HEADS * HEAD_DIM
PEER_HEADS = 8
PEER_NKEYS = 128
PEER_N = PEER_NKEYS * PEER_NKEYS
PEER_HALF = 64
PEER_TOPK = 16
DEPTH = 1
ALPHA = (2.0 * DEPTH) ** 0.25
LN_EPS = 1e-5
QK_SCALE = HEAD_DIM ** -0.5
NEG = -1e30
INV_SQRT2 = 0.7071067811865476

LANES = 128
SUPER_ROW = 2 * GRID_W
NA_WIN = 5
NA_BIAS_SPAN = 9
VMEM_LIMIT = 56 * 1024 * 1024

TOKEN_TILE = 512
ROUTE_TILE = 256
PEER_TILE = 512
PEER_CHUNK = 1024


def _cparams(sem):
    return pltpu.CompilerParams(dimension_semantics=sem, vmem_limit_bytes=VMEM_LIMIT)


def _layer_norm(x, g, b):
    mu = jnp.mean(x, axis=-1, keepdims=True)
    xc = x - mu
    var = jnp.mean(xc * xc, axis=-1, keepdims=True)
    return xc * lax.rsqrt(var + LN_EPS) * g + b


def _dot(a, b):
    return jnp.dot(a, b, preferred_element_type=F32)


def _dot_nt(a, b):
    return lax.dot_general(a, b, (((1,), (1,)), ((), ())), preferred_element_type=F32)


def _in_proj_kernel(x_ref, g_ref, b_ref, wql_ref, wqh_ref, wkt_ref, wv_ref, wp_ref, wml_ref, wmh_ref,
                    xn_ref, ql_ref, qh_ref, kt_ref, v_ref, xp_ref, ml_ref, mh_ref):
    y = _layer_norm(x_ref[...], g_ref[...], b_ref[...])
    xn_ref[...] = y
    yb = y.astype(BF16)
    ql_ref[...] = (_dot(yb, wql_ref[...]) * QK_SCALE).astype(BF16)
    qh_ref[...] = (_dot(yb, wqh_ref[...]) * QK_SCALE).astype(BF16)
    kt_ref[...] = _dot_nt(wkt_ref[...], yb).astype(BF16)
    v_ref[...] = _dot(yb, wv_ref[...]).astype(BF16)
    xp_ref[...] = _dot(yb, wp_ref[...])
    ml_ref[...] = (_dot(yb, wml_ref[...]) * QK_SCALE).astype(BF16)
    mh_ref[...] = (_dot(yb, wmh_ref[...]) * QK_SCALE).astype(BF16)


def _in_proj(x, g, b, wql, wqh, wkt, wv, wp, wml, wmh):
    n = x.shape[0]
    tm = TOKEN_TILE
    row = lambda w: pl.BlockSpec((tm, w), lambda i: (i, 0))
    full = lambda a: pl.BlockSpec(a.shape, lambda i: (0,) * a.ndim)
    return pl.pallas_call(
        _in_proj_kernel,
        grid=(n // tm,),
        in_specs=[row(D_MODEL), full(g), full(b), full(wql), full(wqh), full(wkt), full(wv), full(wp),
                  full(wml), full(wmh)],
        out_specs=[row(D_MODEL), row(NA_WIDTH), row(NA_WIDTH),
                   pl.BlockSpec((NA_WIDTH, tm), lambda i: (0, i)),
                   row(NA_WIDTH), row(POOL_WIDTH), row(MEM_WIDTH), row(MEM_WIDTH)],
        out_shape=[jax.ShapeDtypeStruct((n, D_MODEL), F32),
                   jax.ShapeDtypeStruct((n, NA_WIDTH), BF16),
                   jax.ShapeDtypeStruct((n, NA_WIDTH), BF16),
                   jax.ShapeDtypeStruct((NA_WIDTH, n), BF16),
                   jax.ShapeDtypeStruct((n, NA_WIDTH), BF16),
                   jax.ShapeDtypeStruct((n, POOL_WIDTH), F32),
                   jax.ShapeDtypeStruct((n, MEM_WIDTH), BF16),
                   jax.ShapeDtypeStruct((n, MEM_WIDTH), BF16)],
        compiler_params=_cparams(("parallel",)),
        name="in_proj",
    )(x, g, b, wql, wqh, wkt, wv, wp, wml, wmh)


def _pair_attention(q_lo, q_hi, kt, v, bias_lo, bias_hi):
    outs = []
    for q, bias in ((q_lo, bias_lo), (q_hi, bias_hi)):
        s = _dot(q, kt) + bias
        mx = jnp.max(s, axis=-1, keepdims=True)
        p = jnp.exp(s - mx)
        l = jnp.sum(p, axis=-1, keepdims=True)
        outs.append(_dot(p.astype(BF16), v) * (1.0 / l))
    lane = lax.broadcasted_iota(jnp.int32, outs[0].shape, 1)
    return jnp.where(lane < HEAD_DIM, outs[0], outs[1])


def _na_kernel(ql_ref, qh_ref, *refs, n_sr, rows):
    kt_refs = refs[0:NA_WIN]
    v_refs = refs[NA_WIN:2 * NA_WIN]
    bias_ref = refs[2 * NA_WIN]
    o_ref = refs[2 * NA_WIN + 1]
    kwin, vwin = refs[2 * NA_WIN + 2:]
    r = pl.program_id(1)
    sr0 = jnp.clip(r - 2, 0, n_sr - NA_WIN)
    for s in range(NA_WIN):
        kwin[:, s * SUPER_ROW:(s + 1) * SUPER_ROW] = kt_refs[s][...]
        vwin[s * SUPER_ROW:(s + 1) * SUPER_ROW, :] = v_refs[s][...]
    nk = NA_WIN * SUPER_ROW
    qi = lax.broadcasted_iota(jnp.int32, (SUPER_ROW, nk), 0)
    ki = lax.broadcasted_iota(jnp.int32, (SUPER_ROW, nk), 1)
    qrow = 2 * r + jnp.where(qi >= GRID_W, 1, 0)
    krow = 2 * sr0 + lax.shift_right_logical(ki, 6)
    rs = jnp.clip(qrow - NA_ROWS // 2, 0, rows - NA_ROWS)
    rmask = jnp.where((krow >= rs) & (krow < rs + NA_ROWS), 0.0, NEG).astype(F32)
    boff = pl.multiple_of((sr0 - r + NA_BIAS_SPAN // 2) * SUPER_ROW, SUPER_ROW)
    for m in range(NA_HEADS // 2):
        cols = slice(m * LANES, (m + 1) * LANES)
        b_lo = bias_ref[2 * m, :, pl.ds(boff, nk)] + rmask
        b_hi = bias_ref[2 * m + 1, :, pl.ds(boff, nk)] + rmask
        o = _pair_attention(ql_ref[:, cols], qh_ref[:, cols], kwin[cols, :], vwin[:, cols], b_lo, b_hi)
        o_ref[:, cols] = o.astype(BF16)


def _na_attention(ql, qh, kt, v, bias, batch, rows):
    n = ql.shape[0]
    n_sr = rows // 2
    assert rows >= 2 * NA_WIN and rows % 2 == 0 and n == batch * n_sr * SUPER_ROW
    qspec = pl.BlockSpec((SUPER_ROW, NA_WIDTH), lambda b, r: (b * n_sr + r, 0))

    def win(r):
        return jnp.clip(r - 2, 0, n_sr - NA_WIN)

    kt_specs = [pl.BlockSpec((NA_WIDTH, SUPER_ROW), lambda b, r, s=s: (0, b * n_sr + win(r) + s))
                for s in range(NA_WIN)]
    v_specs = [pl.BlockSpec((SUPER_ROW, NA_WIDTH), lambda b, r, s=s: (b * n_sr + win(r) + s, 0))
               for s in range(NA_WIN)]
    return pl.pallas_call(
        functools.partial(_na_kernel, n_sr=n_sr, rows=rows),
        grid=(batch, n_sr),
        in_specs=[qspec, qspec] + kt_specs + v_specs + [pl.BlockSpec(bias.shape, lambda b, r: (0, 0, 0))],
        out_specs=qspec,
        out_shape=jax.ShapeDtypeStruct((n, NA_WIDTH), BF16),
        scratch_shapes=[pltpu.VMEM((NA_WIDTH, NA_WIN * SUPER_ROW), BF16),
                        pltpu.VMEM((NA_WIN * SUPER_ROW, NA_WIDTH), BF16)],
        compiler_params=_cparams(("parallel", "parallel")),
        name="na_attn",
    )(ql, qh, *([kt] * NA_WIN), *([v] * NA_WIN), bias)


def _na_bias_table(rpb):
    h = rpb.shape[0]
    pad = GRID_W - NA_COLS
    padded = jnp.pad(rpb.astype(F32), ((0, 0), (0, 0), (pad, pad)))
    colexp = jnp.stack([padded[:, :, pad + NA_COLS - 1 - qc:pad + NA_COLS - 1 - qc + GRID_W]
                        for qc in range(GRID_W)], axis=2)
    zeros = jnp.zeros((h, GRID_W, GRID_W), F32)
    per_parity = []
    for qp in range(2):
        slots = []
        for dsr in range(-(NA_BIAS_SPAN // 2), NA_BIAS_SPAN // 2 + 1):
            for kp in range(2):
                r = 2 * dsr + kp - qp + NA_ROWS - 1
                slots.append(colexp[:, r] if 0 <= r <= 2 * NA_ROWS - 2 else zeros)
        per_parity.append(jnp.concatenate(slots, axis=2))
    tab = jnp.concatenate(per_parity, axis=1)
    qc = np.arange(SUPER_ROW) % GRID_W
    kc = np.arange(NA_BIAS_SPAN * SUPER_ROW) % GRID_W
    wstart = np.clip(qc - NA_COLS // 2, 0, GRID_W - NA_COLS)
    col_ok = (kc[None, :] >= wstart[:, None]) & (kc[None, :] < wstart[:, None] + NA_COLS)
    return jnp.where(col_ok[None], tab, NEG)


def _pool_kernel(prev_ref, cur_ref, next_ref, w_ref, sc_ref, o_ref, buf, *, seq, tm):
    i = pl.program_id(1)
    nt = seq // tm
    halo = 8
    zero = jnp.zeros((halo, POOL_WIDTH), F32)
    buf[0:halo, :] = jnp.where(i == 0, zero, prev_ref[...])
    buf[halo:halo + tm, :] = cur_ref[...]
    buf[halo + tm:2 * halo + tm, :] = jnp.where(i == nt - 1, zero, next_ref[...])
    pos = i * tm + lax.broadcasted_iota(jnp.int32, (tm, LANES), 0)
    lane = lax.broadcasted_iota(jnp.int32, (tm, LANES), 1)

    def shifted(d, c):
        return buf[halo + d:halo + d + tm, c * LANES:(c + 1) * LANES]

    def count(w):
        return (jnp.minimum(pos + w // 2, seq) - jnp.maximum(pos - w // 2, 0)).astype(F32)

    x0 = shifted(0, 0)
    s2 = shifted(-1, 0) + x0
    s4 = s2 + shifted(-2, 0) + shifted(1, 0)
    pa = jnp.where(lane < POOL_GROUP, s2 / count(2), s4 / count(4)) - x0
    x1 = shifted(0, 1)
    s8 = x1
    for d in (-4, -3, -2, -1, 1, 2, 3):
        s8 = s8 + shifted(d, 1)
    s16 = s8
    for d in (-8, -7, -6, -5, 4, 5, 6, 7):
        s16 = s16 + shifted(d, 1)
    pb = jnp.where(lane < POOL_GROUP, s8 / count(8), s16 / count(16)) - x1
    pooled = jnp.concatenate([pa, pb], axis=1).astype(BF16)
    o_ref[...] = (_dot(pooled, w_ref[...]) * sc_ref[...]).astype(BF16)


def _pool(xp, w_bd, scale, batch, seq):
    n = xp.shape[0]
    tm = TOKEN_TILE
    nt = seq // tm
    assert seq % tm == 0 and n == batch * seq
    return pl.pallas_call(
        functools.partial(_pool_kernel, seq=seq, tm=tm),
        grid=(batch, nt),
        in_specs=[
            pl.BlockSpec((8, POOL_WIDTH), lambda b, i: (jnp.maximum((b * seq + i * tm) // 8 - 1, 0), 0)),
            pl.BlockSpec((tm, POOL_WIDTH), lambda b, i: (b * nt + i, 0)),
            pl.BlockSpec((8, POOL_WIDTH),
                         lambda b, i: (jnp.minimum((b * seq + (i + 1) * tm) // 8, n // 8 - 1), 0)),
            pl.BlockSpec(w_bd.shape, lambda b, i: (0, 0)),
            pl.BlockSpec(scale.shape, lambda b, i: (0, 0)),
        ],
        out_specs=pl.BlockSpec((tm, POOL_WIDTH), lambda b, i: (b * nt + i, 0)),
        out_shape=jax.ShapeDtypeStruct((n, POOL_WIDTH), BF16),
        scratch_shapes=[pltpu.VMEM((tm + 16, POOL_WIDTH), F32)],
        compiler_params=_cparams(("parallel", "parallel")),
        name="pool",
    )(xp, xp, xp, w_bd, scale)


def _mem_kv_kernel(mem_ref, wkt_ref, wv_ref, kt_ref, v_ref):
    mb = mem_ref[0].astype(BF16)
    kt_ref[0] = _dot_nt(wkt_ref[...], mb).astype(BF16)
    v_ref[0] = _dot(mb, wv_ref[...]).astype(BF16)


def _mem_kv(mem, wkt, wv):
    b, m, _ = mem.shape
    return pl.pallas_call(
        _mem_kv_kernel,
        grid=(b,),
        in_specs=[pl.BlockSpec((1, m, D_MODEL), lambda i: (i, 0, 0)),
                  pl.BlockSpec(wkt.shape, lambda i: (0, 0)),
                  pl.BlockSpec(wv.shape, lambda i: (0, 0))],
        out_specs=[pl.BlockSpec((1, MEM_WIDTH, m), lambda i: (i, 0, 0)),
                   pl.BlockSpec((1, m, MEM_WIDTH), lambda i: (i, 0, 0))],
        out_shape=[jax.ShapeDtypeStruct((b, MEM_WIDTH, m), BF16),
                   jax.ShapeDtypeStruct((b, m, MEM_WIDTH), BF16)],
        compiler_params=_cparams(("parallel",)),
        name="mem_kv",
    )(mem, wkt, wv)


def _mem_attn_kernel(ql_ref, qh_ref, kt_ref, v_ref, o_ref):
    zero = jnp.zeros((1, 1), F32)
    for m in range(MEM_HEADS // 2):
        cols = slice(m * LANES, (m + 1) * LANES)
        o = _pair_attention(ql_ref[:, cols], qh_ref[:, cols], kt_ref[0, cols, :], v_ref[0, :, cols],
                            zero, zero)
        o_ref[:, cols] = o.astype(BF16)


def _mem_attention(ql, qh, kt, v, batch, seq):
    n = ql.shape[0]
    tm = TOKEN_TILE
    nt = seq // tm
    m = kt.shape[2]
    qspec = pl.BlockSpec((tm, MEM_WIDTH), lambda b, i: (b * nt + i, 0))
    return pl.pallas_call(
        _mem_attn_kernel,
        grid=(batch, nt),
        in_specs=[qspec, qspec,
                  pl.BlockSpec((1, MEM_WIDTH, m), lambda b, i: (b, 0, 0)),
                  pl.BlockSpec((1, m, MEM_WIDTH), lambda b, i: (b, 0, 0))],
        out_specs=qspec,
        out_shape=jax.ShapeDtypeStruct((n, MEM_WIDTH), BF16),
        compiler_params=_cparams(("parallel", "parallel")),
        name="mem_attn",
    )(ql, qh, kt, v)


def _out_proj_kernel(xn_ref, na_ref, pool_ref, mem_ref, w1_ref, w2_ref, w3_ref, g_ref, b_ref, o_ref):
    mixed = _dot(na_ref[...], w1_ref[...]) + _dot(pool_ref[...], w2_ref[...]) + _dot(mem_ref[...], w3_ref[...])
    o_ref[...] = _layer_norm(ALPHA * xn_ref[...] + mixed, g_ref[...], b_ref[...])


def _out_proj(xn, y_na, y_pool, y_mem, w1, w2, w3, g, b):
    n = xn.shape[0]
    tm = TOKEN_TILE
    row = lambda w: pl.BlockSpec((tm, w), lambda i: (i, 0))
    full = lambda a: pl.BlockSpec(a.shape, lambda i: (0,) * a.ndim)
    return pl.pallas_call(
        _out_proj_kernel,
        grid=(n // tm,),
        in_specs=[row(D_MODEL), row(NA_WIDTH), row(POOL_WIDTH), row(MEM_WIDTH),
                  full(w1), full(w2), full(w3), full(g), full(b)],
        out_specs=row(D_MODEL),
        out_shape=jax.ShapeDtypeStruct((n, D_MODEL), F32),
        compiler_params=_cparams(("parallel",)),
        name="out_proj",
    )(xn, y_na, y_pool, y_mem, w1, w2, w3, g, b)


def _staircase():
    return [(a, b) for a in range(PEER_TOPK) for b in range(PEER_TOPK) if (a + 1) * (b + 1) <= PEER_TOPK]


def _route_kernel(x_ref, wqt_ref, keys_ref, rank_ref, e1_ref, thr_ref, a_ref, sc, sv):
    xb = x_ref[...].astype(BF16)
    qt = _dot_nt(wqt_ref[...], xb).astype(BF16)
    for hp in range(2 * PEER_HEADS):
        sc[hp] = _dot(keys_ref[hp % 2], qt[hp * PEER_HALF:(hp + 1) * PEER_HALF, :])

    ninf = F32(-jnp.inf)
    for h in range(PEER_HEADS):
        for p in range(2):
            s = sc[2 * h + p]
            rank = jnp.zeros_like(s)
            lt = None
            for k in range(PEER_TOPK):
                cand = s if lt is None else jnp.where(lt, s, ninf)
                m = jnp.max(cand, axis=0, keepdims=True)
                sv[p, k, h:h + 1, :] = m
                lt = s < m
                if p == 1:
                    rank = rank + jnp.where(lt, 1.0, 0.0)
            if p == 1:
                rank_ref[h] = rank

    sv0 = [sv[0, k] for k in range(PEER_TOPK)]
    sv1 = [sv[1, k] for k in range(PEER_TOPK)]
    stair = _staircase()
    cands = [sv0[a] + sv1[b] for a, b in stair]
    top = cands[0]
    tau = top
    for _ in range(PEER_TOPK - 1):
        nxt = None
        for c in cands[1:]:
            v = jnp.where(c < tau, c, ninf)
            nxt = v if nxt is None else jnp.maximum(nxt, v)
        tau = nxt
    z = None
    for c in cands:
        v = jnp.where(c >= tau, jnp.exp(c - top), 0.0)
        z = v if z is None else z + v
    inv_z = 1.0 / z
    partners = [None] * PEER_TOPK
    for (a, b), c in zip(stair, cands):
        v = jnp.where(c >= tau, 1.0, 0.0)
        partners[a] = v if partners[a] is None else partners[a] + v

    for h in range(PEER_HEADS):
        hs = slice(h, h + 1)
        s0 = sc[2 * h]
        s1 = sc[2 * h + 1]
        thr = jnp.zeros_like(s0)
        for a in range(PEER_TOPK):
            thr = jnp.where(s0 == sv0[a][hs], partners[a][hs], thr)
        thr_ref[h] = thr
        a_ref[h] = jnp.exp(s0 - sv0[0][hs]) * inv_z[hs]
        e1_ref[h] = jnp.exp(s1 - sv1[0][hs])


def _route(x1, wqt, keys):
    n = x1.shape[0]
    tr = ROUTE_TILE
    ospec = pl.BlockSpec((PEER_HEADS, PEER_NKEYS, tr), lambda i: (0, 0, i))
    oshape = jax.ShapeDtypeStruct((PEER_HEADS, PEER_NKEYS, n), F32)
    return pl.pallas_call(
        _route_kernel,
        grid=(n // tr,),
        in_specs=[pl.BlockSpec((tr, D_MODEL), lambda i: (i, 0)),
                  pl.BlockSpec(wqt.shape, lambda i: (0, 0)),
                  pl.BlockSpec(keys.shape, lambda i: (0, 0, 0))],
        out_specs=[ospec] * 4,
        out_shape=[oshape] * 4,
        scratch_shapes=[pltpu.VMEM((2 * PEER_HEADS, PEER_NKEYS, tr), F32),
                        pltpu.VMEM((2, PEER_TOPK, PEER_HEADS, tr), F32)],
        compiler_params=_cparams(("parallel",)),
        name="peer_route",
    )(x1, wqt, keys)


_SUB_ROWS = 16
_TILE_BLOCKS = 8
_BLOCKS = PEER_CHUNK // PEER_NKEYS


_N_CHUNKS = PEER_N // PEER_CHUNK


_MM_ROWS = 512
_MM_COLS = 256


def _peer_kernel(x_ref, u_ref, vt_ref, rank_ref, e1_ref, thr_ref, a_ref, g_ref, b_ref, o_ref,
                 xt, ht, wt, acc):
    s = pl.program_id(1)
    tm = x_ref.shape[0]
    n_piece = (PEER_CHUNK // _MM_ROWS) * (tm // _MM_COLS)
    tiles_per_piece = (tm // LANES) * (PEER_NKEYS // _SUB_ROWS) // n_piece
    n_rg = PEER_NKEYS // _SUB_ROWS

    def gate_tile(slot, c0, r0, blk0):
        gates = [jnp.zeros((_SUB_ROWS, LANES), F32) for _ in range(_TILE_BLOCKS)]
        for h in range(PEER_HEADS):
            rk = rank_ref[h, pl.ds(r0, _SUB_ROWS), pl.ds(c0, LANES)]
            e1 = e1_ref[h, pl.ds(r0, _SUB_ROWS), pl.ds(c0, LANES)]
            for i in range(_TILE_BLOCKS):
                thr = thr_ref[h, blk0 + i:blk0 + i + 1, pl.ds(c0, LANES)]
                a = a_ref[h, blk0 + i:blk0 + i + 1, pl.ds(c0, LANES)]
                gates[i] = gates[i] + jnp.where(rk < thr, e1, 0.0) * a
        for i in range(_TILE_BLOCKS):
            rows = pl.ds((blk0 + i) * PEER_NKEYS + r0, _SUB_ROWS)
            hh = ht[slot, rows, pl.ds(c0, LANES)]
            act = 0.5 * hh * (1.0 + lax.erf(hh * INV_SQRT2))
            wt[slot, rows, pl.ds(c0, LANES)] = (gates[i] * act).astype(BF16)

    def step(p, stage1, stage2, stage3):
        def piece(it, carry):
            m0 = pl.multiple_of((it // (tm // _MM_COLS)) * _MM_ROWS, _MM_ROWS)
            n0 = pl.multiple_of((it % (tm // _MM_COLS)) * _MM_COLS, _MM_COLS)
            rows, cols = pl.ds(m0, _MM_ROWS), pl.ds(n0, _MM_COLS)
            if stage3:
                acc[rows, cols] += _dot(vt_ref[rows, :], wt[p, :, cols])
            if stage1:
                ht[p, rows, cols] = _dot(u_ref[rows, :], xt[:, cols])
            if stage2:
                c0 = pl.multiple_of(it * LANES, LANES)
                for r0 in range(0, PEER_NKEYS, _SUB_ROWS):
                    for blk0 in range(0, _BLOCKS, _TILE_BLOCKS):
                        gate_tile(1 - p, c0, r0, blk0)
            return carry

        lax.fori_loop(0, n_piece, piece, 0)

    @pl.when(s == 0)
    def _():
        xt[...] = x_ref[...].T.astype(BF16)
        acc[...] = jnp.zeros_like(acc)
        step(0, True, False, False)

    @pl.when(s == 1)
    def _():
        step(1, True, True, False)

    for p in range(2):
        @pl.when((s >= 2) & (s < _N_CHUNKS) & (s % 2 == p))
        def _(p=p):
            step(p, True, True, True)

    @pl.when(s == _N_CHUNKS)
    def _():
        step(_N_CHUNKS % 2, False, True, True)

    @pl.when(s == _N_CHUNKS + 1)
    def _():
        step((_N_CHUNKS + 1) % 2, False, False, True)
        o_ref[...] = _layer_norm(ALPHA * x_ref[...] + acc[...].T, g_ref[...], b_ref[...])


def _peer(x1, u, vt, rank, e1, thr, a, g, b):
    n = x1.shape[0]
    tm = PEER_TILE
    ec = PEER_CHUNK
    last = _N_CHUNKS - 1
    chunk = lambda j, lag: jnp.clip(j - lag, 0, last)
    dense = pl.BlockSpec((PEER_HEADS, PEER_NKEYS, tm), lambda i, j: (0, 0, i))
    per_blk = pl.BlockSpec((PEER_HEADS, _BLOCKS, tm), lambda i, j: (0, chunk(j, 1), i))
    vec = pl.BlockSpec((1, D_MODEL), lambda i, j: (0, 0))
    return pl.pallas_call(
        _peer_kernel,
        grid=(n // tm, _N_CHUNKS + 2),
        in_specs=[pl.BlockSpec((tm, D_MODEL), lambda i, j: (i, 0)),
                  pl.BlockSpec((ec, D_MODEL), lambda i, j: (chunk(j, 0), 0)),
                  pl.BlockSpec((D_MODEL, ec), lambda i, j: (0, chunk(j, 2))),
                  dense, dense, per_blk, per_blk, vec, vec],
        out_specs=pl.BlockSpec((tm, D_MODEL), lambda i, j: (i, 0)),
        out_shape=jax.ShapeDtypeStruct((n, D_MODEL), F32),
        scratch_shapes=[pltpu.VMEM((D_MODEL, tm), BF16),
                        pltpu.VMEM((2, ec, tm), F32),
                        pltpu.VMEM((2, ec, tm), BF16),
                        pltpu.VMEM((D_MODEL, tm), F32)],
        compiler_params=_cparams(("parallel", "arbitrary")),
        name="peer_dense",
    )(x1, u, vt, rank, e1, thr, a, g, b)


def _head_pair_masks(width):
    head = np.arange(width) // HEAD_DIM
    lo = (head % 2 == 0).astype(np.float32)
    return lo, 1.0 - lo


def _prepare(ln_in_g, ln_in_b, w_in, w_mem_kv, na_rpb, w_pool, pool_scale, w_out, ln1_g, ln1_b,
             w_query, sub_keys, peer_u, peer_v, ln2_g, ln2_b):
    wi = w_in[0]
    o = NA_WIDTH
    wq, wk, wv = wi[:, 0:o], wi[:, o:2 * o], wi[:, 2 * o:3 * o]
    wp = wi[:, 3 * o:3 * o + POOL_WIDTH]
    wm = wi[:, 3 * o + POOL_WIDTH:]
    lo, hi = _head_pair_masks(NA_WIDTH)
    mlo, mhi = _head_pair_masks(MEM_WIDTH)
    w_bd = jnp.zeros((POOL_WIDTH, POOL_WIDTH), F32)
    for g in range(len(POOL_WINDOWS)):
        sl = slice(g * POOL_GROUP, (g + 1) * POOL_GROUP)
        w_bd = w_bd.at[sl, sl].set(w_pool[0, g])
    wkv = w_mem_kv[0]
    wo = w_out[0].astype(BF16)
    row = lambda v: v.reshape(1, -1).astype(F32)
    return dict(
        ln_in_g=row(ln_in_g), ln_in_b=row(ln_in_b),
        wql=(wq * lo).astype(BF16), wqh=(wq * hi).astype(BF16), wkt=wk.T.astype(BF16), wv=wv.astype(BF16),
        wp=wp.astype(BF16), wml=(wm * mlo).astype(BF16), wmh=(wm * mhi).astype(BF16),
        mem_wkt=wkv[:, :MEM_WIDTH].T.astype(BF16), mem_wv=wkv[:, MEM_WIDTH:].astype(BF16),
        na_bias=_na_bias_table(na_rpb[0]),
        w_bd=w_bd.astype(BF16), pool_scale=row(pool_scale[0]),
        wo_na=wo[:NA_WIDTH], wo_pool=wo[NA_WIDTH:NA_WIDTH + POOL_WIDTH], wo_mem=wo[NA_WIDTH + POOL_WIDTH:],
        ln1_g=row(ln1_g[0]), ln1_b=row(ln1_b[0]),
        wqt=w_query[0].T.astype(BF16), keys=sub_keys[0].astype(BF16),
        u=peer_u[0].astype(BF16), vt=peer_v[0].T.astype(BF16),
        ln2_g=row(ln2_g[0]), ln2_b=row(ln2_b[0]),
    )


def _trunk(x, mem, w):
    batch, seq, _ = x.shape
    rows = seq // GRID_W
    xf = x.reshape(batch * seq, D_MODEL)
    xn, ql, qh, kt, v, xp, ml, mh = _in_proj(xf, w["ln_in_g"], w["ln_in_b"], w["wql"], w["wqh"], w["wkt"],
                                             w["wv"], w["wp"], w["wml"], w["wmh"])
    y_na = _na_attention(ql, qh, kt, v, w["na_bias"], batch, rows)
    y_pool = _pool(xp, w["w_bd"], w["pool_scale"], batch, seq)
    mkt, mv = _mem_kv(mem, w["mem_wkt"], w["mem_wv"])
    y_mem = _mem_attention(ml, mh, mkt, mv, batch, seq)
    x1 = _out_proj(xn, y_na, y_pool, y_mem, w["wo_na"], w["wo_pool"], w["wo_mem"], w["ln1_g"], w["ln1_b"])
    rank, e1, thr, a = _route(x1, w["wqt"], w["keys"])
    y = _peer(x1, w["u"], w["vt"], rank, e1, thr, a, w["ln2_g"], w["ln2_b"])
    return y.reshape(batch, seq, D_MODEL)


def kernel(x_prompt, x_sample, mem_prompt, mem_sample, ln_in_g, ln_in_b, w_in, w_mem_kv, na_rpb, w_pool, pool_scale, w_out, ln1_g, ln1_b, w_query, sub_keys, peer_u, peer_v, ln2_g, ln2_b):
    w = _prepare(ln_in_g, ln_in_b, w_in, w_mem_kv, na_rpb, w_pool, pool_scale, w_out, ln1_g, ln1_b,
                 w_query, sub_keys, peer_u, peer_v, ln2_g, ln2_b)
    return (_trunk(x_prompt, mem_prompt, w), _trunk(x_sample, mem_sample, w))
```

```python
import functools

import numpy as np
import jax
import jax.numpy as jnp
from jax import lax
from jax.experimental import pallas as pl
from jax.experimental.pallas import tpu as pltpu

F32 = jnp.float32
BF16 = jnp.bfloat16

D_MODEL = 1024
GRID_W = 64
HEAD_DIM = 64
NA_HEADS = 8
NA_WIDTH = NA_HEADS * HEAD_DIM
NA_ROWS = 8
NA_COLS = 16
POOL_WINDOWS = (2, 4, 8, 16)
POOL_GROUP = 64
POOL_WIDTH = POOL_GROUP * len(POOL_WINDOWS)
MEM_HEADS = 4
MEM_WIDTH = MEM_HEADS * HEAD_DIM
PEER_HEADS = 8
PEER_NKEYS = 128
PEER_N = PEER_NKEYS * PEER_NKEYS
PEER_HALF = 64
PEER_TOPK = 16
DEPTH = 1
ALPHA = (2.0 * DEPTH) ** 0.25
LN_EPS = 1e-5
QK_SCALE = HEAD_DIM ** -0.5
NEG = -1e30
INV_SQRT2 = 0.7071067811865476

LANES = 128
SUPER_ROW = 2 * GRID_W
NA_WIN = 5
NA_BIAS_SPAN = 9
VMEM_LIMIT = 56 * 1024 * 1024

TOKEN_TILE = 512
ROUTE_TILE = 256
PEER_TILE = 512
PEER_CHUNK = 1024


def _cparams(sem):
    return pltpu.CompilerParams(dimension_semantics=sem, vmem_limit_bytes=VMEM_LIMIT)


def _layer_norm(x, g, b):
    mu = jnp.mean(x, axis=-1, keepdims=True)
    xc = x - mu
    var = jnp.mean(xc * xc, axis=-1, keepdims=True)
    return xc * lax.rsqrt(var + LN_EPS) * g + b


def _dot(a, b):
    return jnp.dot(a, b, preferred_element_type=F32)


def _dot_nt(a, b):
    return lax.dot_general(a, b, (((1,), (1,)), ((), ())), preferred_element_type=F32)


def _in_proj_kernel(x_ref, g_ref, b_ref, wql_ref, wqh_ref, wkt_ref, wv_ref, wp_ref, wml_ref, wmh_ref,
                    xn_ref, ql_ref, qh_ref, kt_ref, v_ref, xp_ref, ml_ref, mh_ref):
    y = _layer_norm(x_ref[...], g_ref[...], b_ref[...])
    xn_ref[...] = y
    yb = y.astype(BF16)
    ql_ref[...] = (_dot(yb, wql_ref[...]) * QK_SCALE).astype(BF16)
    qh_ref[...] = (_dot(yb, wqh_ref[...]) * QK_SCALE).astype(BF16)
    kt_ref[...] = _dot_nt(wkt_ref[...], yb).astype(BF16)
    v_ref[...] = _dot(yb, wv_ref[...]).astype(BF16)
    xp_ref[...] = _dot(yb, wp_ref[...])
    ml_ref[...] = (_dot(yb, wml_ref[...]) * QK_SCALE).astype(BF16)
    mh_ref[...] = (_dot(yb, wmh_ref[...]) * QK_SCALE).astype(BF16)


def _in_proj(x, g, b, wql, wqh, wkt, wv, wp, wml, wmh):
    n = x.shape[0]
    tm = TOKEN_TILE
    row = lambda w: pl.BlockSpec((tm, w), lambda i: (i, 0))
    full = lambda a: pl.BlockSpec(a.shape, lambda i: (0,) * a.ndim)
    return pl.pallas_call(
        _in_proj_kernel,
        grid=(n // tm,),
        in_specs=[row(D_MODEL), full(g), full(b), full(wql), full(wqh), full(wkt), full(wv), full(wp),
                  full(wml), full(wmh)],
        out_specs=[row(D_MODEL), row(NA_WIDTH), row(NA_WIDTH),
                   pl.BlockSpec((NA_WIDTH, tm), lambda i: (0, i)),
                   row(NA_WIDTH), row(POOL_WIDTH), row(MEM_WIDTH), row(MEM_WIDTH)],
        out_shape=[jax.ShapeDtypeStruct((n, D_MODEL), F32),
                   jax.ShapeDtypeStruct((n, NA_WIDTH), BF16),
                   jax.ShapeDtypeStruct((n, NA_WIDTH), BF16),
                   jax.ShapeDtypeStruct((NA_WIDTH, n), BF16),
                   jax.ShapeDtypeStruct((n, NA_WIDTH), BF16),
                   jax.ShapeDtypeStruct((n, POOL_WIDTH), F32),
                   jax.ShapeDtypeStruct((n, MEM_WIDTH), BF16),
                   jax.ShapeDtypeStruct((n, MEM_WIDTH), BF16)],
        compiler_params=_cparams(("parallel",)),
        name="in_proj",
    )(x, g, b, wql, wqh, wkt, wv, wp, wml, wmh)


_ATTN_LOOKAHEAD = 2


def _paired_heads_attention(n_heads, q_of, kt_of, v_of, bias_of, store):
    def scores(h):
        return _dot(q_of(h), kt_of(h // 2)) + bias_of(h)

    pending = [scores(h) for h in range(min(_ATTN_LOOKAHEAD, n_heads))]
    outs = []
    for h in range(n_heads):
        s = pending.pop(0)
        if h + _ATTN_LOOKAHEAD < n_heads:
            pending.append(scores(h + _ATTN_LOOKAHEAD))
        mx = jnp.max(s, axis=-1, keepdims=True)
        p = jnp.exp(s - mx)
        l = jnp.sum(p, axis=-1, keepdims=True)
        outs.append(_dot(p.astype(BF16), v_of(h // 2)) * (1.0 / l))
        if h % 2 == 1:
            lane = lax.broadcasted_iota(jnp.int32, outs[0].shape, 1)
            store(h // 2, jnp.where(lane < HEAD_DIM, outs[0], outs[1]))
            outs = []


def _na_kernel(ql_ref, qh_ref, *refs, n_sr, rows):
    kt_refs = refs[0:NA_WIN]
    v_refs = refs[NA_WIN:2 * NA_WIN]
    bias_ref = refs[2 * NA_WIN]
    o_ref = refs[2 * NA_WIN + 1]
    kwin, vwin = refs[2 * NA_WIN + 2:]
    r = pl.program_id(1)
    sr0 = jnp.clip(r - 2, 0, n_sr - NA_WIN)
    for s in range(NA_WIN):
        kwin[:, s * SUPER_ROW:(s + 1) * SUPER_ROW] = kt_refs[s][...]
        vwin[s * SUPER_ROW:(s + 1) * SUPER_ROW, :] = v_refs[s][...]
    nk = NA_WIN * SUPER_ROW
    qi = lax.broadcasted_iota(jnp.int32, (SUPER_ROW, nk), 0)
    ki = lax.broadcasted_iota(jnp.int32, (SUPER_ROW, nk), 1)
    qrow = 2 * r + jnp.where(qi >= GRID_W, 1, 0)
    krow = 2 * sr0 + lax.shift_right_logical(ki, 6)
    rs = jnp.clip(qrow - NA_ROWS // 2, 0, rows - NA_ROWS)
    rmask = jnp.where((krow >= rs) & (krow < rs + NA_ROWS), 0.0, NEG).astype(F32)
    boff = pl.multiple_of((sr0 - r + NA_BIAS_SPAN // 2) * SUPER_ROW, SUPER_ROW)
    cols = lambda m: slice(m * LANES, (m + 1) * LANES)

    def store(m, out):
        o_ref[:, cols(m)] = out.astype(BF16)

    _paired_heads_attention(
        NA_HEADS,
        q_of=lambda h: (ql_ref, qh_ref)[h % 2][:, cols(h // 2)],
        kt_of=lambda m: kwin[cols(m), :],
        v_of=lambda m: vwin[:, cols(m)],
        bias_of=lambda h: bias_ref[h, :, pl.ds(boff, nk)] + rmask,
        store=store)


def _na_attention(ql, qh, kt, v, bias, batch, rows):
    n = ql.shape[0]
    n_sr = rows // 2
    assert rows >= 2 * NA_WIN and rows % 2 == 0 and n == batch * n_sr * SUPER_ROW
    qspec = pl.BlockSpec((SUPER_ROW, NA_WIDTH), lambda b, r: (b * n_sr + r, 0))

    def win(r):
        return jnp.clip(r - 2, 0, n_sr - NA_WIN)

    kt_specs = [pl.BlockSpec((NA_WIDTH, SUPER_ROW), lambda b, r, s=s: (0, b * n_sr + win(r) + s))
                for s in range(NA_WIN)]
    v_specs = [pl.BlockSpec((SUPER_ROW, NA_WIDTH), lambda b, r, s=s: (b * n_sr + win(r) + s, 0))
               for s in range(NA_WIN)]
    return pl.pallas_call(
        functools.partial(_na_kernel, n_sr=n_sr, rows=rows),
        grid=(batch, n_sr),
        in_specs=[qspec, qspec] + kt_specs + v_specs + [pl.BlockSpec(bias.shape, lambda b, r: (0, 0, 0))],
        out_specs=qspec,
        out_shape=jax.ShapeDtypeStruct((n, NA_WIDTH), BF16),
        scratch_shapes=[pltpu.VMEM((NA_WIDTH, NA_WIN * SUPER_ROW), BF16),
                        pltpu.VMEM((NA_WIN * SUPER_ROW, NA_WIDTH), BF16)],
        compiler_params=_cparams(("parallel", "parallel")),
        name="na_attn",
    )(ql, qh, *([kt] * NA_WIN), *([v] * NA_WIN), bias)


def _na_bias_table(rpb):
    h = rpb.shape[0]
    pad = GRID_W - NA_COLS
    padded = jnp.pad(rpb.astype(F32), ((0, 0), (0, 0), (pad, pad)))
    colexp = jnp.stack([padded[:, :, pad + NA_COLS - 1 - qc:pad + NA_COLS - 1 - qc + GRID_W]
                        for qc in range(GRID_W)], axis=2)
    zeros = jnp.zeros((h, GRID_W, GRID_W), F32)
    per_parity = []
    for qp in range(2):
        slots = []
        for dsr in range(-(NA_BIAS_SPAN // 2), NA_BIAS_SPAN // 2 + 1):
            for kp in range(2):
                r = 2 * dsr + kp - qp + NA_ROWS - 1
                slots.append(colexp[:, r] if 0 <= r <= 2 * NA_ROWS - 2 else zeros)
        per_parity.append(jnp.concatenate(slots, axis=2))
    tab = jnp.concatenate(per_parity, axis=1)
    qc = np.arange(SUPER_ROW) % GRID_W
    kc = np.arange(NA_BIAS_SPAN * SUPER_ROW) % GRID_W
    wstart = np.clip(qc - NA_COLS // 2, 0, GRID_W - NA_COLS)
    col_ok = (kc[None, :] >= wstart[:, None]) & (kc[None, :] < wstart[:, None] + NA_COLS)
    return jnp.where(col_ok[None], tab, NEG)


def _pool_kernel(prev_ref, cur_ref, next_ref, w_ref, sc_ref, o_ref, buf, *, seq, tm):
    i = pl.program_id(1)
    nt = seq // tm
    halo = 8
    zero = jnp.zeros((halo, POOL_WIDTH), F32)
    buf[0:halo, :] = jnp.where(i == 0, zero, prev_ref[...])
    buf[halo:halo + tm, :] = cur_ref[...]
    buf[halo + tm:2 * halo + tm, :] = jnp.where(i == nt - 1, zero, next_ref[...])
    pos = i * tm + lax.broadcasted_iota(jnp.int32, (tm, LANES), 0)
    lane = lax.broadcasted_iota(jnp.int32, (tm, LANES), 1)

    def shifted(d, c):
        return buf[halo + d:halo + d + tm, c * LANES:(c + 1) * LANES]

    def count(w):
        return (jnp.minimum(pos + w // 2, seq) - jnp.maximum(pos - w // 2, 0)).astype(F32)

    x0 = shifted(0, 0)
    s2 = shifted(-1, 0) + x0
    s4 = s2 + shifted(-2, 0) + shifted(1, 0)
    pa = jnp.where(lane < POOL_GROUP, s2 / count(2), s4 / count(4)) - x0
    x1 = shifted(0, 1)
    s8 = x1
    for d in (-4, -3, -2, -1, 1, 2, 3):
        s8 = s8 + shifted(d, 1)
    s16 = s8
    for d in (-8, -7, -6, -5, 4, 5, 6, 7):
        s16 = s16 + shifted(d, 1)
    pb = jnp.where(lane < POOL_GROUP, s8 / count(8), s16 / count(16)) - x1
    pooled = jnp.concatenate([pa, pb], axis=1).astype(BF16)
    o_ref[...] = (_dot(pooled, w_ref[...]) * sc_ref[...]).astype(BF16)


def _pool(xp, w_bd, scale, batch, seq):
    n = xp.shape[0]
    tm = TOKEN_TILE
    nt = seq // tm
    assert seq % tm == 0 and n == batch * seq
    return pl.pallas_call(
        functools.partial(_pool_kernel, seq=seq, tm=tm),
        grid=(batch, nt),
        in_specs=[
            pl.BlockSpec((8, POOL_WIDTH), lambda b, i: (jnp.maximum((b * seq + i * tm) // 8 - 1, 0), 0)),
            pl.BlockSpec((tm, POOL_WIDTH), lambda b, i: (b * nt + i, 0)),
            pl.BlockSpec((8, POOL_WIDTH),
                         lambda b, i: (jnp.minimum((b * seq + (i + 1) * tm) // 8, n // 8 - 1), 0)),
            pl.BlockSpec(w_bd.shape, lambda b, i: (0, 0)),
            pl.BlockSpec(scale.shape, lambda b, i: (0, 0)),
        ],
        out_specs=pl.BlockSpec((tm, POOL_WIDTH), lambda b, i: (b * nt + i, 0)),
        out_shape=jax.ShapeDtypeStruct((n, POOL_WIDTH), BF16),
        scratch_shapes=[pltpu.VMEM((tm + 16, POOL_WIDTH), F32)],
        compiler_params=_cparams(("parallel", "parallel")),
        name="pool",
    )(xp, xp, xp, w_bd, scale)


def _mem_kv_kernel(mem_ref, wkt_ref, wv_ref, kt_ref, v_ref):
    mb = mem_ref[0].astype(BF16)
    kt_ref[0] = _dot_nt(wkt_ref[...], mb).astype(BF16)
    v_ref[0] = _dot(mb, wv_ref[...]).astype(BF16)


def _mem_kv(mem, wkt, wv):
    b, m, _ = mem.shape
    return pl.pallas_call(
        _mem_kv_kernel,
        grid=(b,),
        in_specs=[pl.BlockSpec((1, m, D_MODEL), lambda i: (i, 0, 0)),
                  pl.BlockSpec(wkt.shape, lambda i: (0, 0)),
                  pl.BlockSpec(wv.shape, lambda i: (0, 0))],
        out_specs=[pl.BlockSpec((1, MEM_WIDTH, m), lambda i: (i, 0, 0)),
                   pl.BlockSpec((1, m, MEM_WIDTH), lambda i: (i, 0, 0))],
        out_shape=[jax.ShapeDtypeStruct((b, MEM_WIDTH, m), BF16),
                   jax.ShapeDtypeStruct((b, m, MEM_WIDTH), BF16)],
        compiler_params=_cparams(("parallel",)),
        name="mem_kv",
    )(mem, wkt, wv)


def _mem_attn_kernel(ql_ref, qh_ref, kt_ref, v_ref, o_ref):
    cols = lambda m: slice(m * LANES, (m + 1) * LANES)
    zero = jnp.zeros((1, 1), F32)

    def store(m, out):
        o_ref[:, cols(m)] = out.astype(BF16)

    _paired_heads_attention(
        MEM_HEADS,
        q_of=lambda h: (ql_ref, qh_ref)[h % 2][:, cols(h // 2)],
        kt_of=lambda m: kt_ref[0, cols(m), :],
        v_of=lambda m: v_ref[0, :, cols(m)],
        bias_of=lambda h: zero,
        store=store)


def _mem_attention(ql, qh, kt, v, batch, seq):
    n = ql.shape[0]
    tm = TOKEN_TILE
    nt = seq // tm
    m = kt.shape[2]
    qspec = pl.BlockSpec((tm, MEM_WIDTH), lambda b, i: (b * nt + i, 0))
    return pl.pallas_call(
        _mem_attn_kernel,
        grid=(batch, nt),
        in_specs=[qspec, qspec,
                  pl.BlockSpec((1, MEM_WIDTH, m), lambda b, i: (b, 0, 0)),
                  pl.BlockSpec((1, m, MEM_WIDTH), lambda b, i: (b, 0, 0))],
        out_specs=qspec,
        out_shape=jax.ShapeDtypeStruct((n, MEM_WIDTH), BF16),
        compiler_params=_cparams(("parallel", "parallel")),
        name="mem_attn",
    )(ql, qh, kt, v)


def _out_proj_kernel(xn_ref, na_ref, pool_ref, mem_ref, w1_ref, w2_ref, w3_ref, g_ref, b_ref, o_ref):
    mixed = _dot(na_ref[...], w1_ref[...]) + _dot(pool_ref[...], w2_ref[...]) + _dot(mem_ref[...], w3_ref[...])
    o_ref[...] = _layer_norm(ALPHA * xn_ref[...] + mixed, g_ref[...], b_ref[...])


def _out_proj(xn, y_na, y_pool, y_mem, w1, w2, w3, g, b):
    n = xn.shape[0]
    tm = TOKEN_TILE
    row = lambda w: pl.BlockSpec((tm, w), lambda i: (i, 0))
    full = lambda a: pl.BlockSpec(a.shape, lambda i: (0,) * a.ndim)
    return pl.pallas_call(
        _out_proj_kernel,
        grid=(n // tm,),
        in_specs=[row(D_MODEL), row(NA_WIDTH), row(POOL_WIDTH), row(MEM_WIDTH),
                  full(w1), full(w2), full(w3), full(g), full(b)],
        out_specs=row(D_MODEL),
        out_shape=jax.ShapeDtypeStruct((n, D_MODEL), F32),
        compiler_params=_cparams(("parallel",)),
        name="out_proj",
    )(xn, y_na, y_pool, y_mem, w1, w2, w3, g, b)


def _staircase():
    return [(a, b) for a in range(PEER_TOPK) for b in range(PEER_TOPK) if (a + 1) * (b + 1) <= PEER_TOPK]


def _route_kernel(x_ref, wqt_ref, keys_ref, s1_ref, e1_ref, phi_ref, a_ref, sc, sv):
    xb = x_ref[...].astype(BF16)
    qt = _dot_nt(wqt_ref[...], xb).astype(BF16)
    for hp in range(2 * PEER_HEADS):
        sc[hp] = _dot(keys_ref[hp % 2], qt[hp * PEER_HALF:(hp + 1) * PEER_HALF, :])

    ninf = F32(-jnp.inf)
    for h in range(PEER_HEADS):
        for p in range(2):
            s = sc[2 * h + p]
            m = None
            for k in range(PEER_TOPK):
                cand = s if m is None else jnp.where(s < m, s, ninf)
                m = jnp.max(cand, axis=0, keepdims=True)
                sv[p, k, h:h + 1, :] = m

    sv0 = [sv[0, k] for k in range(PEER_TOPK)]
    sv1 = [sv[1, k] for k in range(PEER_TOPK)]
    stair = _staircase()
    cands = [sv0[a] + sv1[b] for a, b in stair]
    top = cands[0]
    tau = top
    for _ in range(PEER_TOPK - 1):
        nxt = None
        for c in cands[1:]:
            v = jnp.where(c < tau, c, ninf)
            nxt = v if nxt is None else jnp.maximum(nxt, v)
        tau = nxt
    z = None
    for c in cands:
        v = jnp.where(c >= tau, jnp.exp(c - top), 0.0)
        z = v if z is None else z + v
    half_inv_z = 0.5 / z
    pinf = F32(jnp.inf)
    floor = [None] * PEER_TOPK
    for (a, b), c in zip(stair, cands):
        v = jnp.where(c >= tau, sv1[b], pinf)
        floor[a] = v if floor[a] is None else jnp.minimum(floor[a], v)

    for h in range(PEER_HEADS):
        hs = slice(h, h + 1)
        s0 = sc[2 * h]
        s1 = sc[2 * h + 1]
        phi = jnp.full_like(s0, pinf)
        for a in range(PEER_TOPK):
            phi = jnp.where(s0 == sv0[a][hs], floor[a][hs], phi)
        phi_ref[h] = phi
        a_ref[h] = jnp.exp(s0 - sv0[0][hs]) * half_inv_z[hs]
        s1_ref[h] = s1
        e1_ref[h] = jnp.exp(s1 - sv1[0][hs])


def _route(x1, wqt, keys):
    n = x1.shape[0]
    tr = ROUTE_TILE
    ospec = pl.BlockSpec((PEER_HEADS, PEER_NKEYS, tr), lambda i: (0, 0, i))
    oshape = jax.ShapeDtypeStruct((PEER_HEADS, PEER_NKEYS, n), F32)
    return pl.pallas_call(
        _route_kernel,
        grid=(n // tr,),
        in_specs=[pl.BlockSpec((tr, D_MODEL), lambda i: (i, 0)),
                  pl.BlockSpec(wqt.shape, lambda i: (0, 0)),
                  pl.BlockSpec(keys.shape, lambda i: (0, 0, 0))],
        out_specs=[ospec] * 4,
        out_shape=[oshape] * 4,
        scratch_shapes=[pltpu.VMEM((2 * PEER_HEADS, PEER_NKEYS, tr), F32),
                        pltpu.VMEM((2, PEER_TOPK, PEER_HEADS, tr), F32)],
        compiler_params=_cparams(("parallel",)),
        name="peer_route",
    )(x1, wqt, keys)


_SUB_ROWS = 16
_TILE_BLOCKS = 8
_BLOCKS = PEER_CHUNK // PEER_NKEYS
_N_CHUNKS = PEER_N // PEER_CHUNK
_MM_ROWS = 512
_MM_COLS = 256


def _peer_kernel(x_ref, u_ref, vt_ref, s1_ref, e1_ref, phi_ref, a_ref, g_ref, b_ref, o_ref,
                 xt, ht, wt, acc):
    s = pl.program_id(1)
    tm = x_ref.shape[0]
    n_piece = (PEER_CHUNK // _MM_ROWS) * (tm // _MM_COLS)
    assert n_piece == tm // LANES

    def gate_tile(slot, c0, r0, blk0):
        gates = [jnp.zeros((_SUB_ROWS, LANES), F32) for _ in range(_TILE_BLOCKS)]
        for h in range(PEER_HEADS):
            s1 = s1_ref[h, pl.ds(r0, _SUB_ROWS), pl.ds(c0, LANES)]
            e1 = e1_ref[h, pl.ds(r0, _SUB_ROWS), pl.ds(c0, LANES)]
            for i in range(_TILE_BLOCKS):
                phi = phi_ref[h, blk0 + i:blk0 + i + 1, pl.ds(c0, LANES)]
                a = a_ref[h, blk0 + i:blk0 + i + 1, pl.ds(c0, LANES)]
                gates[i] = gates[i] + jnp.where(s1 >= phi, e1, 0.0) * a
        for i in range(_TILE_BLOCKS):
            rows = pl.ds((blk0 + i) * PEER_NKEYS + r0, _SUB_ROWS)
            hh = ht[slot, rows, pl.ds(c0, LANES)]
            act = hh * (1.0 + lax.erf(hh * INV_SQRT2))
            wt[slot, rows, pl.ds(c0, LANES)] = (gates[i] * act).astype(BF16)

    def step(p, stage1, stage2, stage3):
        def piece(it, carry):
            m0 = pl.multiple_of((it // (tm // _MM_COLS)) * _MM_ROWS, _MM_ROWS)
            n0 = pl.multiple_of((it % (tm // _MM_COLS)) * _MM_COLS, _MM_COLS)
            rows, cols = pl.ds(m0, _MM_ROWS), pl.ds(n0, _MM_COLS)
            if stage3:
                acc[rows, cols] += _dot(vt_ref[rows, :], wt[p, :, cols])
            if stage1:
                ht[p, rows, cols] = _dot(u_ref[rows, :], xt[:, cols])
            if stage2:
                c0 = pl.multiple_of(it * LANES, LANES)
                for r0 in range(0, PEER_NKEYS, _SUB_ROWS):
                    for blk0 in range(0, _BLOCKS, _TILE_BLOCKS):
                        gate_tile(1 - p, c0, r0, blk0)
            return carry

        lax.fori_loop(0, n_piece, piece, 0)

    @pl.when(s == 0)
    def _():
        xt[...] = x_ref[...].T.astype(BF16)
        acc[...] = jnp.zeros_like(acc)
        step(0, True, False, False)

    @pl.when(s == 1)
    def _():
        step(1, True, True, False)

    for p in range(2):
        @pl.when((s >= 2) & (s < _N_CHUNKS) & (s % 2 == p))
        def _(p=p):
            step(p, True, True, True)

    @pl.when(s == _N_CHUNKS)
    def _():
        step(_N_CHUNKS % 2, False, True, True)

    @pl.when(s == _N_CHUNKS + 1)
    def _():
        step((_N_CHUNKS + 1) % 2, False, False, True)
        o_ref[...] = _layer_norm(ALPHA * x_ref[...] + acc[...].T, g_ref[...], b_ref[...])


def _peer(x1, u, vt, s1, e1, phi, a, g, b):
    n = x1.shape[0]
    tm = PEER_TILE
    ec = PEER_CHUNK
    last = _N_CHUNKS - 1
    chunk = lambda j, lag: jnp.clip(j - lag, 0, last)
    dense = pl.BlockSpec((PEER_HEADS, PEER_NKEYS, tm), lambda i, j: (0, 0, i))
    per_blk = pl.BlockSpec((PEER_HEADS, _BLOCKS, tm), lambda i, j: (0, chunk(j, 1), i))
    vec = pl.BlockSpec((1, D_MODEL), lambda i, j: (0, 0))
    return pl.pallas_call(
        _peer_kernel,
        grid=(n // tm, _N_CHUNKS + 2),
        in_specs=[pl.BlockSpec((tm, D_MODEL), lambda i, j: (i, 0)),
                  pl.BlockSpec((ec, D_MODEL), lambda i, j: (chunk(j, 0), 0)),
                  pl.BlockSpec((D_MODEL, ec), lambda i, j: (0, chunk(j, 2))),
                  dense, dense, per_blk, per_blk, vec, vec],
        out_specs=pl.BlockSpec((tm, D_MODEL), lambda i, j: (i, 0)),
        out_shape=jax.ShapeDtypeStruct((n, D_MODEL), F32),
        scratch_shapes=[pltpu.VMEM((D_MODEL, tm), BF16),
                        pltpu.VMEM((2, ec, tm), F32),
                        pltpu.VMEM((2, ec, tm), BF16),
                        pltpu.VMEM((D_MODEL, tm), F32)],
        compiler_params=_cparams(("parallel", "arbitrary")),
        name="peer_dense",
    )(x1, u, vt, s1, e1, phi, a, g, b)


def _head_pair_masks(width):
    head = np.arange(width) // HEAD_DIM
    lo = (head % 2 == 0).astype(np.float32)
    return lo, 1.0 - lo


def _prepare(ln_in_g, ln_in_b, w_in, w_mem_kv, na_rpb, w_pool, pool_scale, w_out, ln1_g, ln1_b,
             w_query, sub_keys, peer_u, peer_v, ln2_g, ln2_b):
    wi = w_in[0]
    o = NA_WIDTH
    wq, wk, wv = wi[:, 0:o], wi[:, o:2 * o], wi[:, 2 * o:3 * o]
    wp = wi[:, 3 * o:3 * o + POOL_WIDTH]
    wm = wi[:, 3 * o + POOL_WIDTH:]
    lo, hi = _head_pair_masks(NA_WIDTH)
    mlo, mhi = _head_pair_masks(MEM_WIDTH)
    w_bd = jnp.zeros((POOL_WIDTH, POOL_WIDTH), F32)
    for g in range(len(POOL_WINDOWS)):
        sl = slice(g * POOL_GROUP, (g + 1) * POOL_GROUP)
        w_bd = w_bd.at[sl, sl].set(w_pool[0, g])
    wkv = w_mem_kv[0]
    wo = w_out[0].astype(BF16)
    row = lambda v: v.reshape(1, -1).astype(F32)
    return dict(
        ln_in_g=row(ln_in_g), ln_in_b=row(ln_in_b),
        wql=(wq * lo).astype(BF16), wqh=(wq * hi).astype(BF16), wkt=wk.T.astype(BF16), wv=wv.astype(BF16),
        wp=wp.astype(BF16), wml=(wm * mlo).astype(BF16), wmh=(wm * mhi).astype(BF16),
        mem_wkt=wkv[:, :MEM_WIDTH].T.astype(BF16), mem_wv=wkv[:, MEM_WIDTH:].astype(BF16),
        na_bias=_na_bias_table(na_rpb[0]),
        w_bd=w_bd.astype(BF16), pool_scale=row(pool_scale[0]),
        wo_na=wo[:NA_WIDTH], wo_pool=wo[NA_WIDTH:NA_WIDTH + POOL_WIDTH], wo_mem=wo[NA_WIDTH + POOL_WIDTH:],
        ln1_g=row(ln1_g[0]), ln1_b=row(ln1_b[0]),
        wqt=w_query[0].T.astype(BF16), keys=sub_keys[0].astype(BF16),
        u=peer_u[0].astype(BF16), vt=peer_v[0].T.astype(BF16),
        ln2_g=row(ln2_g[0]), ln2_b=row(ln2_b[0]),
    )


def _trunk(x, mem, w):
    batch, seq, _ = x.shape
    rows = seq // GRID_W
    xf = x.reshape(batch * seq, D_MODEL)
    xn, ql, qh, kt, v, xp, ml, mh = _in_proj(xf, w["ln_in_g"], w["ln_in_b"], w["wql"], w["wqh"], w["wkt"],
                                             w["wv"], w["wp"], w["wml"], w["wmh"])
    y_na = _na_attention(ql, qh, kt, v, w["na_bias"], batch, rows)
    y_pool = _pool(xp, w["w_bd"], w["pool_scale"], batch, seq)
    mkt, mv = _mem_kv(mem, w["mem_wkt"], w["mem_wv"])
    y_mem = _mem_attention(ml, mh, mkt, mv, batch, seq)
    x1 = _out_proj(xn, y_na, y_pool, y_mem, w["wo_na"], w["wo_pool"], w["wo_mem"], w["ln1_g"], w["ln1_b"])
    s1, e1, phi, a = _route(x1, w["wqt"], w["keys"])
    y = _peer(x1, w["u"], w["vt"], s1, e1, phi, a, w["ln2_g"], w["ln2_b"])
    return y.reshape(batch, seq, D_MODEL)


def kernel(x_prompt, x_sample, mem_prompt, mem_sample, ln_in_g, ln_in_b, w_in, w_mem_kv, na_rpb, w_pool, pool_scale, w_out, ln1_g, ln1_b, w_query, sub_keys, peer_u, peer_v, ln2_g, ln2_b):
    w = _prepare(ln_in_g, ln_in_b, w_in, w_mem_kv, na_rpb, w_pool, pool_scale, w_out, ln1_g, ln1_b,
                 w_query, sub_keys, peer_u, peer_v, ln2_g, ln2_b)
    return (_trunk(x_prompt, mem_prompt, w), _trunk(x_sample, mem_sample, w))
```

```python
import functools

import numpy as np
import jax
import jax.numpy as jnp
from jax import lax
from jax.experimental import pallas as pl
from jax.experimental.pallas import tpu as pltpu

F32 = jnp.float32
BF16 = jnp.bfloat16

D_MODEL = 1024
GRID_W = 64
HEAD_DIM = 64
NA_HEADS = 8
NA_WIDTH = NA_HEADS * HEAD_DIM
NA_ROWS = 8
NA_COLS = 16
POOL_WINDOWS = (2, 4, 8, 16)
POOL_GROUP = 64
POOL_WIDTH = POOL_GROUP * len(POOL_WINDOWS)
MEM_HEADS = 4
MEM_WIDTH = MEM_HEADS * HEAD_DIM
PEER_HEADS = 8
PEER_NKEYS = 128
PEER_N = PEER_NKEYS * PEER_NKEYS
PEER_HALF = 64
PEER_TOPK = 16
DEPTH = 1
ALPHA = (2.0 * DEPTH) ** 0.25
LN_EPS = 1e-5
QK_SCALE = HEAD_DIM ** -0.5
NEG = -1e30
INV_SQRT2 = 0.7071067811865476

LANES = 128
SUPER_ROW = 2 * GRID_W
NA_WIN = 5
NA_BIAS_SPAN = 9
VMEM_LIMIT = 56 * 1024 * 1024

TOKEN_TILE = 512
ROUTE_TILE = 256
PEER_TILE = 512
PEER_CHUNK = 1024


def _cparams(sem):
    return pltpu.CompilerParams(dimension_semantics=sem, vmem_limit_bytes=VMEM_LIMIT)


def _layer_norm(x, g, b):
    mu = jnp.mean(x, axis=-1, keepdims=True)
    xc = x - mu
    var = jnp.mean(xc * xc, axis=-1, keepdims=True)
    return xc * lax.rsqrt(var + LN_EPS) * g + b


def _dot(a, b):
    return jnp.dot(a, b, preferred_element_type=F32)


def _dot_nt(a, b):
    return lax.dot_general(a, b, (((1,), (1,)), ((), ())), preferred_element_type=F32)


def _in_proj_kernel(x_ref, g_ref, b_ref, wql_ref, wqh_ref, wkt_ref, wv_ref, wp_ref, wml_ref, wmh_ref,
                    xn_ref, ql_ref, qh_ref, kt_ref, v_ref, xp_ref, ml_ref, mh_ref):
    y = _layer_norm(x_ref[...], g_ref[...], b_ref[...])
    xn_ref[...] = y
    yb = y.astype(BF16)
    ql_ref[...] = (_dot(yb, wql_ref[...]) * QK_SCALE).astype(BF16)
    qh_ref[...] = (_dot(yb, wqh_ref[...]) * QK_SCALE).astype(BF16)
    kt_ref[...] = _dot_nt(wkt_ref[...], yb).astype(BF16)
    v_ref[...] = _dot(yb, wv_ref[...]).astype(BF16)
    xp_ref[...] = _dot(yb, wp_ref[...])
    ml_ref[...] = (_dot(yb, wml_ref[...]) * QK_SCALE).astype(BF16)
    mh_ref[...] = (_dot(yb, wmh_ref[...]) * QK_SCALE).astype(BF16)


def _in_proj(x, g, b, wql, wqh, wkt, wv, wp, wml, wmh):
    n = x.shape[0]
    tm = TOKEN_TILE
    row = lambda w: pl.BlockSpec((tm, w), lambda i: (i, 0))
    full = lambda a: pl.BlockSpec(a.shape, lambda i: (0,) * a.ndim)
    return pl.pallas_call(
        _in_proj_kernel,
        grid=(n // tm,),
        in_specs=[row(D_MODEL), full(g), full(b), full(wql), full(wqh), full(wkt), full(wv), full(wp),
                  full(wml), full(wmh)],
        out_specs=[row(D_MODEL), row(NA_WIDTH), row(NA_WIDTH),
                   pl.BlockSpec((NA_WIDTH, tm), lambda i: (0, i)),
                   row(NA_WIDTH), row(POOL_WIDTH), row(MEM_WIDTH), row(MEM_WIDTH)],
        out_shape=[jax.ShapeDtypeStruct((n, D_MODEL), F32),
                   jax.ShapeDtypeStruct((n, NA_WIDTH), BF16),
                   jax.ShapeDtypeStruct((n, NA_WIDTH), BF16),
                   jax.ShapeDtypeStruct((NA_WIDTH, n), BF16),
                   jax.ShapeDtypeStruct((n, NA_WIDTH), BF16),
                   jax.ShapeDtypeStruct((n, POOL_WIDTH), F32),
                   jax.ShapeDtypeStruct((n, MEM_WIDTH), BF16),
                   jax.ShapeDtypeStruct((n, MEM_WIDTH), BF16)],
        compiler_params=_cparams(("parallel",)),
        name="in_proj",
    )(x, g, b, wql, wqh, wkt, wv, wp, wml, wmh)


_ATTN_LOOKAHEAD = 2


def _paired_heads_attention(n_heads, q_of, kt_of, v_of, bias_of, store):
    def scores(h):
        return _dot(q_of(h), kt_of(h // 2)) + bias_of(h)

    pending = [scores(h) for h in range(min(_ATTN_LOOKAHEAD, n_heads))]
    outs = []
    for h in range(n_heads):
        s = pending.pop(0)
        if h + _ATTN_LOOKAHEAD < n_heads:
            pending.append(scores(h + _ATTN_LOOKAHEAD))
        mx = jnp.max(s, axis=-1, keepdims=True)
        p = jnp.exp(s - mx)
        l = jnp.sum(p, axis=-1, keepdims=True)
        outs.append(_dot(p.astype(BF16), v_of(h // 2)) * (1.0 / l))
        if h % 2 == 1:
            lane = lax.broadcasted_iota(jnp.int32, outs[0].shape, 1)
            store(h // 2, jnp.where(lane < HEAD_DIM, outs[0], outs[1]))
            outs = []


def _na_kernel(ql_ref, qh_ref, *refs, n_sr, rows):
    kt_refs = refs[0:NA_WIN]
    v_refs = refs[NA_WIN:2 * NA_WIN]
    bias_ref = refs[2 * NA_WIN]
    o_ref = refs[2 * NA_WIN + 1]
    kwin, vwin = refs[2 * NA_WIN + 2:]
    r = pl.program_id(1)
    sr0 = jnp.clip(r - 2, 0, n_sr - NA_WIN)
    for s in range(NA_WIN):
        kwin[:, s * SUPER_ROW:(s + 1) * SUPER_ROW] = kt_refs[s][...]
        vwin[s * SUPER_ROW:(s + 1) * SUPER_ROW, :] = v_refs[s][...]
    nk = NA_WIN * SUPER_ROW
    qi = lax.broadcasted_iota(jnp.int32, (SUPER_ROW, nk), 0)
    ki = lax.broadcasted_iota(jnp.int32, (SUPER_ROW, nk), 1)
    qrow = 2 * r + jnp.where(qi >= GRID_W, 1, 0)
    krow = 2 * sr0 + lax.shift_right_logical(ki, 6)
    rs = jnp.clip(qrow - NA_ROWS // 2, 0, rows - NA_ROWS)
    rmask = jnp.where((krow >= rs) & (krow < rs + NA_ROWS), 0.0, NEG).astype(F32)
    boff = pl.multiple_of((sr0 - r + NA_BIAS_SPAN // 2) * SUPER_ROW, SUPER_ROW)
    cols = lambda m: slice(m * LANES, (m + 1) * LANES)

    def store(m, out):
        o_ref[:, cols(m)] = out.astype(BF16)

    _paired_heads_attention(
        NA_HEADS,
        q_of=lambda h: (ql_ref, qh_ref)[h % 2][:, cols(h // 2)],
        kt_of=lambda m: kwin[cols(m), :],
        v_of=lambda m: vwin[:, cols(m)],
        bias_of=lambda h: bias_ref[h, :, pl.ds(boff, nk)] + rmask,
        store=store)


def _na_attention(ql, qh, kt, v, bias, batch, rows):
    n = ql.shape[0]
    n_sr = rows // 2
    assert rows >= 2 * NA_WIN and rows % 2 == 0 and n == batch * n_sr * SUPER_ROW
    qspec = pl.BlockSpec((SUPER_ROW, NA_WIDTH), lambda b, r: (b * n_sr + r, 0))

    def win(r):
        return jnp.clip(r - 2, 0, n_sr - NA_WIN)

    kt_specs = [pl.BlockSpec((NA_WIDTH, SUPER_ROW), lambda b, r, s=s: (0, b * n_sr + win(r) + s))
                for s in range(NA_WIN)]
    v_specs = [pl.BlockSpec((SUPER_ROW, NA_WIDTH), lambda b, r, s=s: (b * n_sr + win(r) + s, 0))
               for s in range(NA_WIN)]
    return pl.pallas_call(
        functools.partial(_na_kernel, n_sr=n_sr, rows=rows),
        grid=(batch, n_sr),
        in_specs=[qspec, qspec] + kt_specs + v_specs + [pl.BlockSpec(bias.shape, lambda b, r: (0, 0, 0))],
        out_specs=qspec,
        out_shape=jax.ShapeDtypeStruct((n, NA_WIDTH), BF16),
        scratch_shapes=[pltpu.VMEM((NA_WIDTH, NA_WIN * SUPER_ROW), BF16),
                        pltpu.VMEM((NA_WIN * SUPER_ROW, NA_WIDTH), BF16)],
        compiler_params=_cparams(("parallel", "parallel")),
        name="na_attn",
    )(ql, qh, *([kt] * NA_WIN), *([v] * NA_WIN), bias)


def _na_bias_table(rpb):
    h = rpb.shape[0]
    pad = GRID_W - NA_COLS
    padded = jnp.pad(rpb.astype(F32), ((0, 0), (0, 0), (pad, pad)))
    colexp = jnp.stack([padded[:, :, pad + NA_COLS - 1 - qc:pad + NA_COLS - 1 - qc + GRID_W]
                        for qc in range(GRID_W)], axis=2)
    zeros = jnp.zeros((h, GRID_W, GRID_W), F32)
    per_parity = []
    for qp in range(2):
        slots = []
        for dsr in range(-(NA_BIAS_SPAN // 2), NA_BIAS_SPAN // 2 + 1):
            for kp in range(2):
                r = 2 * dsr + kp - qp + NA_ROWS - 1
                slots.append(colexp[:, r] if 0 <= r <= 2 * NA_ROWS - 2 else zeros)
        per_parity.append(jnp.concatenate(slots, axis=2))
    tab = jnp.concatenate(per_parity, axis=1)
    qc = np.arange(SUPER_ROW) % GRID_W
    kc = np.arange(NA_BIAS_SPAN * SUPER_ROW) % GRID_W
    wstart = np.clip(qc - NA_COLS // 2, 0, GRID_W - NA_COLS)
    col_ok = (kc[None, :] >= wstart[:, None]) & (kc[None, :] < wstart[:, None] + NA_COLS)
    return jnp.where(col_ok[None], tab, NEG)


def _pool_kernel(prev_ref, cur_ref, next_ref, w_ref, sc_ref, o_ref, buf, *, seq, tm):
    i = pl.program_id(1)
    nt = seq // tm
    halo = 8
    zero = jnp.zeros((halo, POOL_WIDTH), F32)
    buf[0:halo, :] = jnp.where(i == 0, zero, prev_ref[...])
    buf[halo:halo + tm, :] = cur_ref[...]
    buf[halo + tm:2 * halo + tm, :] = jnp.where(i == nt - 1, zero, next_ref[...])
    pos = i * tm + lax.broadcasted_iota(jnp.int32, (tm, LANES), 0)
    lane = lax.broadcasted_iota(jnp.int32, (tm, LANES), 1)

    def shifted(d, c):
        return buf[halo + d:halo + d + tm, c * LANES:(c + 1) * LANES]

    def count(w):
        return (jnp.minimum(pos + w // 2, seq) - jnp.maximum(pos - w // 2, 0)).astype(F32)

    x0 = shifted(0, 0)
    s2 = shifted(-1, 0) + x0
    s4 = s2 + shifted(-2, 0) + shifted(1, 0)
    pa = jnp.where(lane < POOL_GROUP, s2 / count(2), s4 / count(4)) - x0
    x1 = shifted(0, 1)
    s8 = x1
    for d in (-4, -3, -2, -1, 1, 2, 3):
        s8 = s8 + shifted(d, 1)
    s16 = s8
    for d in (-8, -7, -6, -5, 4, 5, 6, 7):
        s16 = s16 + shifted(d, 1)
    pb = jnp.where(lane < POOL_GROUP, s8 / count(8), s16 / count(16)) - x1
    pooled = jnp.concatenate([pa, pb], axis=1).astype(BF16)
    o_ref[...] = (_dot(pooled, w_ref[...]) * sc_ref[...]).astype(BF16)


def _pool(xp, w_bd, scale, batch, seq):
    n = xp.shape[0]
    tm = TOKEN_TILE
    nt = seq // tm
    assert seq % tm == 0 and n == batch * seq
    return pl.pallas_call(
        functools.partial(_pool_kernel, seq=seq, tm=tm),
        grid=(batch, nt),
        in_specs=[
            pl.BlockSpec((8, POOL_WIDTH), lambda b, i: (jnp.maximum((b * seq + i * tm) // 8 - 1, 0), 0)),
            pl.BlockSpec((tm, POOL_WIDTH), lambda b, i: (b * nt + i, 0)),
            pl.BlockSpec((8, POOL_WIDTH),
                         lambda b, i: (jnp.minimum((b * seq + (i + 1) * tm) // 8, n // 8 - 1), 0)),
            pl.BlockSpec(w_bd.shape, lambda b, i: (0, 0)),
            pl.BlockSpec(scale.shape, lambda b, i: (0, 0)),
        ],
        out_specs=pl.BlockSpec((tm, POOL_WIDTH), lambda b, i: (b * nt + i, 0)),
        out_shape=jax.ShapeDtypeStruct((n, POOL_WIDTH), BF16),
        scratch_shapes=[pltpu.VMEM((tm + 16, POOL_WIDTH), F32)],
        compiler_params=_cparams(("parallel", "parallel")),
        name="pool",
    )(xp, xp, xp, w_bd, scale)


def _mem_kv_kernel(mem_ref, wkt_ref, wv_ref, kt_ref, v_ref):
    mb = mem_ref[0].astype(BF16)
    kt_ref[0] = _dot_nt(wkt_ref[...], mb).astype(BF16)
    v_ref[0] = _dot(mb, wv_ref[...]).astype(BF16)


def _mem_kv(mem, wkt, wv):
    b, m, _ = mem.shape
    return pl.pallas_call(
        _mem_kv_kernel,
        grid=(b,),
        in_specs=[pl.BlockSpec((1, m, D_MODEL), lambda i: (i, 0, 0)),
                  pl.BlockSpec(wkt.shape, lambda i: (0, 0)),
                  pl.BlockSpec(wv.shape, lambda i: (0, 0))],
        out_specs=[pl.BlockSpec((1, MEM_WIDTH, m), lambda i: (i, 0, 0)),
                   pl.BlockSpec((1, m, MEM_WIDTH), lambda i: (i, 0, 0))],
        out_shape=[jax.ShapeDtypeStruct((b, MEM_WIDTH, m), BF16),
                   jax.ShapeDtypeStruct((b, m, MEM_WIDTH), BF16)],
        compiler_params=_cparams(("parallel",)),
        name="mem_kv",
    )(mem, wkt, wv)


def _mem_attn_kernel(ql_ref, qh_ref, kt_ref, v_ref, o_ref):
    cols = lambda m: slice(m * LANES, (m + 1) * LANES)
    zero = jnp.zeros((1, 1), F32)

    def store(m, out):
        o_ref[:, cols(m)] = out.astype(BF16)

    _paired_heads_attention(
        MEM_HEADS,
        q_of=lambda h: (ql_ref, qh_ref)[h % 2][:, cols(h // 2)],
        kt_of=lambda m: kt_ref[0, cols(m), :],
        v_of=lambda m: v_ref[0, :, cols(m)],
        bias_of=lambda h: zero,
        store=store)


def _mem_attention(ql, qh, kt, v, batch, seq):
    n = ql.shape[0]
    tm = TOKEN_TILE
    nt = seq // tm
    m = kt.shape[2]
    qspec = pl.BlockSpec((tm, MEM_WIDTH), lambda b, i: (b * nt + i, 0))
    return pl.pallas_call(
        _mem_attn_kernel,
        grid=(batch, nt),
        in_specs=[qspec, qspec,
                  pl.BlockSpec((1, MEM_WIDTH, m), lambda b, i: (b, 0, 0)),
                  pl.BlockSpec((1, m, MEM_WIDTH), lambda b, i: (b, 0, 0))],
        out_specs=qspec,
        out_shape=jax.ShapeDtypeStruct((n, MEM_WIDTH), BF16),
        compiler_params=_cparams(("parallel", "parallel")),
        name="mem_attn",
    )(ql, qh, kt, v)


def _out_proj_kernel(xn_ref, na_ref, pool_ref, mem_ref, w1_ref, w2_ref, w3_ref, g_ref, b_ref, o_ref):
    mixed = _dot(na_ref[...], w1_ref[...]) + _dot(pool_ref[...], w2_ref[...]) + _dot(mem_ref[...], w3_ref[...])
    o_ref[...] = _layer_norm(ALPHA * xn_ref[...] + mixed, g_ref[...], b_ref[...])


def _out_proj(xn, y_na, y_pool, y_mem, w1, w2, w3, g, b):
    n = xn.shape[0]
    tm = TOKEN_TILE
    row = lambda w: pl.BlockSpec((tm, w), lambda i: (i, 0))
    full = lambda a: pl.BlockSpec(a.shape, lambda i: (0,) * a.ndim)
    return pl.pallas_call(
        _out_proj_kernel,
        grid=(n // tm,),
        in_specs=[row(D_MODEL), row(NA_WIDTH), row(POOL_WIDTH), row(MEM_WIDTH),
                  full(w1), full(w2), full(w3), full(g), full(b)],
        out_specs=row(D_MODEL),
        out_shape=jax.ShapeDtypeStruct((n, D_MODEL), F32),
        compiler_params=_cparams(("parallel",)),
        name="out_proj",
    )(xn, y_na, y_pool, y_mem, w1, w2, w3, g, b)


def _staircase():
    return [(a, b) for a in range(PEER_TOPK) for b in range(PEER_TOPK) if (a + 1) * (b + 1) <= PEER_TOPK]


def _route_kernel(x_ref, wqt_ref, keys_ref, s1_ref, e1_ref, phi_ref, a_ref, sc, sv):
    xb = x_ref[...].astype(BF16)
    qt = _dot_nt(wqt_ref[...], xb).astype(BF16)
    for hp in range(2 * PEER_HEADS):
        sc[hp] = _dot(keys_ref[hp % 2], qt[hp * PEER_HALF:(hp + 1) * PEER_HALF, :])

    ninf = F32(-jnp.inf)
    for h in range(PEER_HEADS):
        for p in range(2):
            s = sc[2 * h + p]
            m = None
            for k in range(PEER_TOPK):
                cand = s if m is None else jnp.where(s < m, s, ninf)
                m = jnp.max(cand, axis=0, keepdims=True)
                sv[p, k, h:h + 1, :] = m

    sv0 = [sv[0, k] for k in range(PEER_TOPK)]
    sv1 = [sv[1, k] for k in range(PEER_TOPK)]
    stair = _staircase()
    cands = [sv0[a] + sv1[b] for a, b in stair]
    top = cands[0]
    tau = top
    for _ in range(PEER_TOPK - 1):
        nxt = None
        for c in cands[1:]:
            v = jnp.where(c < tau, c, ninf)
            nxt = v if nxt is None else jnp.maximum(nxt, v)
        tau = nxt
    z = None
    for c in cands:
        v = jnp.where(c >= tau, jnp.exp(c - top), 0.0)
        z = v if z is None else z + v
    half_inv_z = 0.5 / z
    pinf = F32(jnp.inf)
    floor = [None] * PEER_TOPK
    for (a, b), c in zip(stair, cands):
        v = jnp.where(c >= tau, sv1[b], pinf)
        floor[a] = v if floor[a] is None else jnp.minimum(floor[a], v)

    for h in range(PEER_HEADS):
        hs = slice(h, h + 1)
        s0 = sc[2 * h]
        s1 = sc[2 * h + 1]
        phi = jnp.full_like(s0, pinf)
        for a in range(PEER_TOPK):
            phi = jnp.where(s0 == sv0[a][hs], floor[a][hs], phi)
        phi_ref[h] = phi
        a_ref[h] = jnp.exp(s0 - sv0[0][hs]) * half_inv_z[hs]
        s1_ref[h] = s1
        e1_ref[h] = jnp.exp(s1 - sv1[0][hs])


def _route(x1, wqt, keys):
    n = x1.shape[0]
    tr = ROUTE_TILE
    ospec = pl.BlockSpec((PEER_HEADS, PEER_NKEYS, tr), lambda i: (0, 0, i))
    oshape = jax.ShapeDtypeStruct((PEER_HEADS, PEER_NKEYS, n), F32)
    return pl.pallas_call(
        _route_kernel,
        grid=(n // tr,),
        in_specs=[pl.BlockSpec((tr, D_MODEL), lambda i: (i, 0)),
                  pl.BlockSpec(wqt.shape, lambda i: (0, 0)),
                  pl.BlockSpec(keys.shape, lambda i: (0, 0, 0))],
        out_specs=[ospec] * 4,
        out_shape=[oshape] * 4,
        scratch_shapes=[pltpu.VMEM((2 * PEER_HEADS, PEER_NKEYS, tr), F32),
                        pltpu.VMEM((2, PEER_TOPK, PEER_HEADS, tr), F32)],
        compiler_params=_cparams(("parallel",)),
        name="peer_route",
    )(x1, wqt, keys)


_SUB_ROWS = 16
_TILE_BLOCKS = 8
_BLOCKS = PEER_CHUNK // PEER_NKEYS
_N_CHUNKS = PEER_N // PEER_CHUNK
_MM_ROWS = 512
_MM_COLS = 256


def _peer_kernel(x_ref, u_ref, vt_ref, s1_ref, e1_ref, phi_ref, a_ref, g_ref, b_ref, o_ref,
                 xt, ht0, ht1, wt0, wt1, acc):
    s = pl.program_id(1)
    tm = x_ref.shape[0]
    ht = (ht0, ht1)
    wt = (wt0, wt1)
    n_piece = (PEER_CHUNK // _MM_ROWS) * (tm // _MM_COLS)
    assert n_piece == tm // LANES

    def gate_tile(slot, c0, r0, blk0):
        gates = [jnp.zeros((_SUB_ROWS, LANES), F32) for _ in range(_TILE_BLOCKS)]
        for h in range(PEER_HEADS):
            s1 = s1_ref[h, pl.ds(r0, _SUB_ROWS), pl.ds(c0, LANES)]
            e1 = e1_ref[h, pl.ds(r0, _SUB_ROWS), pl.ds(c0, LANES)]
            for i in range(_TILE_BLOCKS):
                phi = phi_ref[h, blk0 + i:blk0 + i + 1, pl.ds(c0, LANES)]
                a = a_ref[h, blk0 + i:blk0 + i + 1, pl.ds(c0, LANES)]
                gates[i] = gates[i] + jnp.where(s1 >= phi, e1, 0.0) * a
        for i in range(_TILE_BLOCKS):
            rows = pl.ds((blk0 + i) * PEER_NKEYS + r0, _SUB_ROWS)
            hh = ht[slot][rows, pl.ds(c0, LANES)]
            act = hh * (1.0 + lax.erf(hh * INV_SQRT2))
            wt[slot][rows, pl.ds(c0, LANES)] = (gates[i] * act).astype(BF16)

    def step(p, stage1, stage2, stage3):
        def piece(it, carry):
            m0 = pl.multiple_of((it // (tm // _MM_COLS)) * _MM_ROWS, _MM_ROWS)
            n0 = pl.multiple_of((it % (tm // _MM_COLS)) * _MM_COLS, _MM_COLS)
            rows, cols = pl.ds(m0, _MM_ROWS), pl.ds(n0, _MM_COLS)
            packed_rows = pl.ds(pl.multiple_of(m0 // 2, _MM_ROWS // 2), _MM_ROWS // 2)
            if stage3:
                acc[rows, cols] += _dot(pltpu.bitcast(vt_ref[packed_rows, :], BF16), wt[p][:, cols])
            if stage1:
                ht[p][rows, cols] = _dot(pltpu.bitcast(u_ref[packed_rows, :], BF16), xt[:, cols])
            if stage2:
                c0 = pl.multiple_of(it * LANES, LANES)
                for r0 in range(0, PEER_NKEYS, _SUB_ROWS):
                    for blk0 in range(0, _BLOCKS, _TILE_BLOCKS):
                        gate_tile(1 - p, c0, r0, blk0)
            return carry

        lax.fori_loop(0, n_piece, piece, 0)

    @pl.when(s == 0)
    def _():
        xt[...] = x_ref[...].T.astype(BF16)
        acc[...] = jnp.zeros_like(acc)
        step(0, True, False, False)

    @pl.when(s == 1)
    def _():
        step(1, True, True, False)

    for p in range(2):
        @pl.when((s >= 2) & (s < _N_CHUNKS) & (s % 2 == p))
        def _(p=p):
            step(p, True, True, True)

    @pl.when(s == _N_CHUNKS)
    def _():
        step(_N_CHUNKS % 2, False, True, True)

    @pl.when(s == _N_CHUNKS + 1)
    def _():
        step((_N_CHUNKS + 1) % 2, False, False, True)
        o_ref[...] = _layer_norm(ALPHA * x_ref[...] + acc[...].T, g_ref[...], b_ref[...])


def _peer(x1, u, vt, s1, e1, phi, a, g, b):
    n = x1.shape[0]
    tm = PEER_TILE
    ec = PEER_CHUNK
    last = _N_CHUNKS - 1
    chunk = lambda j, lag: jnp.clip(j - lag, 0, last)
    dense = pl.BlockSpec((PEER_HEADS, PEER_NKEYS, tm), lambda i, j: (0, 0, i))
    per_blk = pl.BlockSpec((PEER_HEADS, _BLOCKS, tm), lambda i, j: (0, chunk(j, 1), i))
    vec = pl.BlockSpec((1, D_MODEL), lambda i, j: (0, 0))
    return pl.pallas_call(
        _peer_kernel,
        grid=(n // tm, _N_CHUNKS + 2),
        in_specs=[pl.BlockSpec((tm, D_MODEL), lambda i, j: (i, 0)),
                  pl.BlockSpec((ec // 2, D_MODEL), lambda i, j: (chunk(j, 0), 0)),
                  pl.BlockSpec((D_MODEL // 2, ec), lambda i, j: (0, chunk(j, 2))),
                  dense, dense, per_blk, per_blk, vec, vec],
        out_specs=pl.BlockSpec((tm, D_MODEL), lambda i, j: (i, 0)),
        out_shape=jax.ShapeDtypeStruct((n, D_MODEL), F32),
        scratch_shapes=[pltpu.VMEM((D_MODEL, tm), BF16),
                        pltpu.VMEM((ec, tm), F32), pltpu.VMEM((ec, tm), F32),
                        pltpu.VMEM((ec, tm), BF16), pltpu.VMEM((ec, tm), BF16),
                        pltpu.VMEM((D_MODEL, tm), F32)],
        compiler_params=_cparams(("parallel", "arbitrary")),
        name="peer_dense",
    )(x1, u, vt, s1, e1, phi, a, g, b)


def _pack_row_pairs(x):
    bits = lax.bitcast_convert_type(x, jnp.uint16).astype(jnp.uint32)
    return bits[0::2] | (bits[1::2] << 16)


def _head_pair_masks(width):
    head = np.arange(width) // HEAD_DIM
    lo = (head % 2 == 0).astype(np.float32)
    return lo, 1.0 - lo


def _prepare(ln_in_g, ln_in_b, w_in, w_mem_kv, na_rpb, w_pool, pool_scale, w_out, ln1_g, ln1_b,
             w_query, sub_keys, peer_u, peer_v, ln2_g, ln2_b):
    wi = w_in[0]
    o = NA_WIDTH
    wq, wk, wv = wi[:, 0:o], wi[:, o:2 * o], wi[:, 2 * o:3 * o]
    wp = wi[:, 3 * o:3 * o + POOL_WIDTH]
    wm = wi[:, 3 * o + POOL_WIDTH:]
    lo, hi = _head_pair_masks(NA_WIDTH)
    mlo, mhi = _head_pair_masks(MEM_WIDTH)
    w_bd = jnp.zeros((POOL_WIDTH, POOL_WIDTH), F32)
    for g in range(len(POOL_WINDOWS)):
        sl = slice(g * POOL_GROUP, (g + 1) * POOL_GROUP)
        w_bd = w_bd.at[sl, sl].set(w_pool[0, g])
    wkv = w_mem_kv[0]
    wo = w_out[0].astype(BF16)
    row = lambda v: v.reshape(1, -1).astype(F32)
    return dict(
        ln_in_g=row(ln_in_g), ln_in_b=row(ln_in_b),
        wql=(wq * lo).astype(BF16), wqh=(wq * hi).astype(BF16), wkt=wk.T.astype(BF16), wv=wv.astype(BF16),
        wp=wp.astype(BF16), wml=(wm * mlo).astype(BF16), wmh=(wm * mhi).astype(BF16),
        mem_wkt=wkv[:, :MEM_WIDTH].T.astype(BF16), mem_wv=wkv[:, MEM_WIDTH:].astype(BF16),
        na_bias=_na_bias_table(na_rpb[0]),
        w_bd=w_bd.astype(BF16), pool_scale=row(pool_scale[0]),
        wo_na=wo[:NA_WIDTH], wo_pool=wo[NA_WIDTH:NA_WIDTH + POOL_WIDTH], wo_mem=wo[NA_WIDTH + POOL_WIDTH:],
        ln1_g=row(ln1_g[0]), ln1_b=row(ln1_b[0]),
        wqt=w_query[0].T.astype(BF16), keys=sub_keys[0].astype(BF16),
        u=_pack_row_pairs(peer_u[0].astype(BF16)), vt=_pack_row_pairs(peer_v[0].T.astype(BF16)),
        ln2_g=row(ln2_g[0]), ln2_b=row(ln2_b[0]),
    )


def _trunk(x, mem, w):
    batch, seq, _ = x.shape
    rows = seq // GRID_W
    xf = x.reshape(batch * seq, D_MODEL)
    xn, ql, qh, kt, v, xp, ml, mh = _in_proj(xf, w["ln_in_g"], w["ln_in_b"], w["wql"], w["wqh"], w["wkt"],
                                             w["wv"], w["wp"], w["wml"], w["wmh"])
    y_na = _na_attention(ql, qh, kt, v, w["na_bias"], batch, rows)
    y_pool = _pool(xp, w["w_bd"], w["pool_scale"], batch, seq)
    mkt, mv = _mem_kv(mem, w["mem_wkt"], w["mem_wv"])
    y_mem = _mem_attention(ml, mh, mkt, mv, batch, seq)
    x1 = _out_proj(xn, y_na, y_pool, y_mem, w["wo_na"], w["wo_pool"], w["wo_mem"], w["ln1_g"], w["ln1_b"])
    s1, e1, phi, a = _route(x1, w["wqt"], w["keys"])
    y = _peer(x1, w["u"], w["vt"], s1, e1, phi, a, w["ln2_g"], w["ln2_b"])
    return y.reshape(batch, seq, D_MODEL)


def kernel(x_prompt, x_sample, mem_prompt, mem_sample, ln_in_g, ln_in_b, w_in, w_mem_kv, na_rpb, w_pool, pool_scale, w_out, ln1_g, ln1_b, w_query, sub_keys, peer_u, peer_v, ln2_g, ln2_b):
    w = _prepare(ln_in_g, ln_in_b, w_in, w_mem_kv, na_rpb, w_pool, pool_scale, w_out, ln1_g, ln1_b,
                 w_query, sub_keys, peer_u, peer_v, ln2_g, ln2_b)
    return (_trunk(x_prompt, mem_prompt, w), _trunk(x_sample, mem_sample, w))
```

```python
import functools

import numpy as np
import jax
import jax.numpy as jnp
from jax import lax
from jax.experimental import pallas as pl
from jax.experimental.pallas import tpu as pltpu

F32 = jnp.float32
BF16 = jnp.bfloat16

D_MODEL = 1024
GRID_W = 64
HEAD_DIM = 64
NA_HEADS = 8
NA_WIDTH = NA_HEADS * HEAD_DIM
NA_ROWS = 8
NA_COLS = 16
POOL_WINDOWS = (2, 4, 8, 16)
POOL_GROUP = 64
POOL_WIDTH = POOL_GROUP * len(POOL_WINDOWS)
MEM_HEADS = 4
MEM_WIDTH = MEM_HEADS * HEAD_DIM
PEER_HEADS = 8
PEER_NKEYS = 128
PEER_N = PEER_NKEYS * PEER_NKEYS
PEER_HALF = 64
PEER_TOPK = 16
DEPTH = 1
ALPHA = (2.0 * DEPTH) ** 0.25
LN_EPS = 1e-5
QK_SCALE = HEAD_DIM ** -0.5
NEG = -1e30
INV_SQRT2 = 0.7071067811865476

LANES = 128
SUPER_ROW = 2 * GRID_W
NA_WIN = 5
NA_BIAS_SPAN = 9
VMEM_LIMIT = 56 * 1024 * 1024

TOKEN_TILE = 512
ROUTE_TILE = 256
PEER_TILE = 512
PEER_CHUNK = 1024


def _cparams(sem):
    return pltpu.CompilerParams(dimension_semantics=sem, vmem_limit_bytes=VMEM_LIMIT)


def _layer_norm(x, g, b):
    mu = jnp.mean(x, axis=-1, keepdims=True)
    xc = x - mu
    var = jnp.mean(xc * xc, axis=-1, keepdims=True)
    return xc * lax.rsqrt(var + LN_EPS) * g + b


def _dot(a, b):
    return jnp.dot(a, b, preferred_element_type=F32)


def _dot_nt(a, b):
    return lax.dot_general(a, b, (((1,), (1,)), ((), ())), preferred_element_type=F32)


def _in_proj_kernel(x_ref, g_ref, b_ref, wql_ref, wqh_ref, wkt_ref, wv_ref, wp_ref, wml_ref, wmh_ref,
                    xn_ref, ql_ref, qh_ref, kt_ref, v_ref, xp_ref, ml_ref, mh_ref):
    y = _layer_norm(x_ref[...], g_ref[...], b_ref[...])
    xn_ref[...] = y
    yb = y.astype(BF16)
    ql_ref[...] = (_dot(yb, wql_ref[...]) * QK_SCALE).astype(BF16)
    qh_ref[...] = (_dot(yb, wqh_ref[...]) * QK_SCALE).astype(BF16)
    kt_ref[...] = _dot_nt(wkt_ref[...], yb).astype(BF16)
    v_ref[...] = _dot(yb, wv_ref[...]).astype(BF16)
    xp_ref[...] = _dot(yb, wp_ref[...])
    ml_ref[...] = (_dot(yb, wml_ref[...]) * QK_SCALE).astype(BF16)
    mh_ref[...] = (_dot(yb, wmh_ref[...]) * QK_SCALE).astype(BF16)


def _in_proj(x, g, b, wql, wqh, wkt, wv, wp, wml, wmh):
    n = x.shape[0]
    tm = TOKEN_TILE
    row = lambda w: pl.BlockSpec((tm, w), lambda i: (i, 0))
    full = lambda a: pl.BlockSpec(a.shape, lambda i: (0,) * a.ndim)
    return pl.pallas_call(
        _in_proj_kernel,
        grid=(n // tm,),
        in_specs=[row(D_MODEL), full(g), full(b), full(wql), full(wqh), full(wkt), full(wv), full(wp),
                  full(wml), full(wmh)],
        out_specs=[row(D_MODEL), row(NA_WIDTH), row(NA_WIDTH),
                   pl.BlockSpec((NA_WIDTH, tm), lambda i: (0, i)),
                   row(NA_WIDTH), row(POOL_WIDTH), row(MEM_WIDTH), row(MEM_WIDTH)],
        out_shape=[jax.ShapeDtypeStruct((n, D_MODEL), F32),
                   jax.ShapeDtypeStruct((n, NA_WIDTH), BF16),
                   jax.ShapeDtypeStruct((n, NA_WIDTH), BF16),
                   jax.ShapeDtypeStruct((NA_WIDTH, n), BF16),
                   jax.ShapeDtypeStruct((n, NA_WIDTH), BF16),
                   jax.ShapeDtypeStruct((n, POOL_WIDTH), F32),
                   jax.ShapeDtypeStruct((n, MEM_WIDTH), BF16),
                   jax.ShapeDtypeStruct((n, MEM_WIDTH), BF16)],
        compiler_params=_cparams(("parallel",)),
        name="in_proj",
    )(x, g, b, wql, wqh, wkt, wv, wp, wml, wmh)


_ATTN_LOOKAHEAD = 2


def _paired_heads_attention(n_heads, q_of, kt_of, v_of, bias_of, store):
    def scores(h):
        return _dot(q_of(h), kt_of(h // 2)) + bias_of(h)

    pending = [scores(h) for h in range(min(_ATTN_LOOKAHEAD, n_heads))]
    outs = []
    for h in range(n_heads):
        s = pending.pop(0)
        if h + _ATTN_LOOKAHEAD < n_heads:
            pending.append(scores(h + _ATTN_LOOKAHEAD))
        mx = jnp.max(s, axis=-1, keepdims=True)
        p = jnp.exp(s - mx)
        l = jnp.sum(p, axis=-1, keepdims=True)
        outs.append(_dot(p.astype(BF16), v_of(h // 2)) * (1.0 / l))
        if h % 2 == 1:
            lane = lax.broadcasted_iota(jnp.int32, outs[0].shape, 1)
            store(h // 2, jnp.where(lane < HEAD_DIM, outs[0], outs[1]))
            outs = []


def _na_kernel(ql_ref, qh_ref, *refs, n_sr, rows):
    kt_refs = refs[0:NA_WIN]
    v_refs = refs[NA_WIN:2 * NA_WIN]
    bias_ref = refs[2 * NA_WIN]
    o_ref = refs[2 * NA_WIN + 1]
    kwin, vwin = refs[2 * NA_WIN + 2:]
    r = pl.program_id(1)
    sr0 = jnp.clip(r - 2, 0, n_sr - NA_WIN)
    for s in range(NA_WIN):
        kwin[:, s * SUPER_ROW:(s + 1) * SUPER_ROW] = kt_refs[s][...]
        vwin[s * SUPER_ROW:(s + 1) * SUPER_ROW, :] = v_refs[s][...]
    nk = NA_WIN * SUPER_ROW
    qi = lax.broadcasted_iota(jnp.int32, (SUPER_ROW, nk), 0)
    ki = lax.broadcasted_iota(jnp.int32, (SUPER_ROW, nk), 1)
    qrow = 2 * r + jnp.where(qi >= GRID_W, 1, 0)
    krow = 2 * sr0 + lax.shift_right_logical(ki, 6)
    rs = jnp.clip(qrow - NA_ROWS // 2, 0, rows - NA_ROWS)
    rmask = jnp.where((krow >= rs) & (krow < rs + NA_ROWS), 0.0, NEG).astype(F32)
    boff = pl.multiple_of((sr0 - r + NA_BIAS_SPAN // 2) * SUPER_ROW, SUPER_ROW)
    cols = lambda m: slice(m * LANES, (m + 1) * LANES)

    def store(m, out):
        o_ref[:, cols(m)] = out.astype(BF16)

    _paired_heads_attention(
        NA_HEADS,
        q_of=lambda h: (ql_ref, qh_ref)[h % 2][:, cols(h // 2)],
        kt_of=lambda m: kwin[cols(m), :],
        v_of=lambda m: vwin[:, cols(m)],
        bias_of=lambda h: bias_ref[h, :, pl.ds(boff, nk)] + rmask,
        store=store)


def _na_attention(ql, qh, kt, v, bias, batch, rows):
    n = ql.shape[0]
    n_sr = rows // 2
    assert rows >= 2 * NA_WIN and rows % 2 == 0 and n == batch * n_sr * SUPER_ROW
    qspec = pl.BlockSpec((SUPER_ROW, NA_WIDTH), lambda b, r: (b * n_sr + r, 0))

    def win(r):
        return jnp.clip(r - 2, 0, n_sr - NA_WIN)

    kt_specs = [pl.BlockSpec((NA_WIDTH, SUPER_ROW), lambda b, r, s=s: (0, b * n_sr + win(r) + s))
                for s in range(NA_WIN)]
    v_specs = [pl.BlockSpec((SUPER_ROW, NA_WIDTH), lambda b, r, s=s: (b * n_sr + win(r) + s, 0))
               for s in range(NA_WIN)]
    return pl.pallas_call(
        functools.partial(_na_kernel, n_sr=n_sr, rows=rows),
        grid=(batch, n_sr),
        in_specs=[qspec, qspec] + kt_specs + v_specs + [pl.BlockSpec(bias.shape, lambda b, r: (0, 0, 0))],
        out_specs=qspec,
        out_shape=jax.ShapeDtypeStruct((n, NA_WIDTH), BF16),
        scratch_shapes=[pltpu.VMEM((NA_WIDTH, NA_WIN * SUPER_ROW), BF16),
                        pltpu.VMEM((NA_WIN * SUPER_ROW, NA_WIDTH), BF16)],
        compiler_params=_cparams(("parallel", "parallel")),
        name="na_attn",
    )(ql, qh, *([kt] * NA_WIN), *([v] * NA_WIN), bias)


def _na_bias_table(rpb):
    h = rpb.shape[0]
    pad = GRID_W - NA_COLS
    padded = jnp.pad(rpb.astype(F32), ((0, 0), (0, 0), (pad, pad)))
    colexp = jnp.stack([padded[:, :, pad + NA_COLS - 1 - qc:pad + NA_COLS - 1 - qc + GRID_W]
                        for qc in range(GRID_W)], axis=2)
    zeros = jnp.zeros((h, GRID_W, GRID_W), F32)
    per_parity = []
    for qp in range(2):
        slots = []
        for dsr in range(-(NA_BIAS_SPAN // 2), NA_BIAS_SPAN // 2 + 1):
            for kp in range(2):
                r = 2 * dsr + kp - qp + NA_ROWS - 1
                slots.append(colexp[:, r] if 0 <= r <= 2 * NA_ROWS - 2 else zeros)
        per_parity.append(jnp.concatenate(slots, axis=2))
    tab = jnp.concatenate(per_parity, axis=1)
    qc = np.arange(SUPER_ROW) % GRID_W
    kc = np.arange(NA_BIAS_SPAN * SUPER_ROW) % GRID_W
    wstart = np.clip(qc - NA_COLS // 2, 0, GRID_W - NA_COLS)
    col_ok = (kc[None, :] >= wstart[:, None]) & (kc[None, :] < wstart[:, None] + NA_COLS)
    return jnp.where(col_ok[None], tab, NEG)


def _pool_kernel(prev_ref, cur_ref, next_ref, w_ref, sc_ref, o_ref, buf, *, seq, tm):
    i = pl.program_id(1)
    nt = seq // tm
    halo = 8
    zero = jnp.zeros((halo, POOL_WIDTH), F32)
    buf[0:halo, :] = jnp.where(i == 0, zero, prev_ref[...])
    buf[halo:halo + tm, :] = cur_ref[...]
    buf[halo + tm:2 * halo + tm, :] = jnp.where(i == nt - 1, zero, next_ref[...])
    pos = i * tm + lax.broadcasted_iota(jnp.int32, (tm, LANES), 0)
    lane = lax.broadcasted_iota(jnp.int32, (tm, LANES), 1)

    def shifted(d, c):
        return buf[halo + d:halo + d + tm, c * LANES:(c + 1) * LANES]

    def count(w):
        return (jnp.minimum(pos + w // 2, seq) - jnp.maximum(pos - w // 2, 0)).astype(F32)

    x0 = shifted(0, 0)
    s2 = shifted(-1, 0) + x0
    s4 = s2 + shifted(-2, 0) + shifted(1, 0)
    pa = jnp.where(lane < POOL_GROUP, s2 / count(2), s4 / count(4)) - x0
    x1 = shifted(0, 1)
    s8 = x1
    for d in (-4, -3, -2, -1, 1, 2, 3):
        s8 = s8 + shifted(d, 1)
    s16 = s8
    for d in (-8, -7, -6, -5, 4, 5, 6, 7):
        s16 = s16 + shifted(d, 1)
    pb = jnp.where(lane < POOL_GROUP, s8 / count(8), s16 / count(16)) - x1
    pooled = jnp.concatenate([pa, pb], axis=1).astype(BF16)
    o_ref[...] = (_dot(pooled, w_ref[...]) * sc_ref[...]).astype(BF16)


def _pool(xp, w_bd, scale, batch, seq):
    n = xp.shape[0]
    tm = TOKEN_TILE
    nt = seq // tm
    assert seq % tm == 0 and n == batch * seq
    return pl.pallas_call(
        functools.partial(_pool_kernel, seq=seq, tm=tm),
        grid=(batch, nt),
        in_specs=[
            pl.BlockSpec((8, POOL_WIDTH), lambda b, i: (jnp.maximum((b * seq + i * tm) // 8 - 1, 0), 0)),
            pl.BlockSpec((tm, POOL_WIDTH), lambda b, i: (b * nt + i, 0)),
            pl.BlockSpec((8, POOL_WIDTH),
                         lambda b, i: (jnp.minimum((b * seq + (i + 1) * tm) // 8, n // 8 - 1), 0)),
            pl.BlockSpec(w_bd.shape, lambda b, i: (0, 0)),
            pl.BlockSpec(scale.shape, lambda b, i: (0, 0)),
        ],
        out_specs=pl.BlockSpec((tm, POOL_WIDTH), lambda b, i: (b * nt + i, 0)),
        out_shape=jax.ShapeDtypeStruct((n, POOL_WIDTH), BF16),
        scratch_shapes=[pltpu.VMEM((tm + 16, POOL_WIDTH), F32)],
        compiler_params=_cparams(("parallel", "parallel")),
        name="pool",
    )(xp, xp, xp, w_bd, scale)


def _mem_kv_kernel(mem_ref, wkt_ref, wv_ref, kt_ref, v_ref):
    mb = mem_ref[0].astype(BF16)
    kt_ref[0] = _dot_nt(wkt_ref[...], mb).astype(BF16)
    v_ref[0] = _dot(mb, wv_ref[...]).astype(BF16)


def _mem_kv(mem, wkt, wv):
    b, m, _ = mem.shape
    return pl.pallas_call(
        _mem_kv_kernel,
        grid=(b,),
        in_specs=[pl.BlockSpec((1, m, D_MODEL), lambda i: (i, 0, 0)),
                  pl.BlockSpec(wkt.shape, lambda i: (0, 0)),
                  pl.BlockSpec(wv.shape, lambda i: (0, 0))],
        out_specs=[pl.BlockSpec((1, MEM_WIDTH, m), lambda i: (i, 0, 0)),
                   pl.BlockSpec((1, m, MEM_WIDTH), lambda i: (i, 0, 0))],
        out_shape=[jax.ShapeDtypeStruct((b, MEM_WIDTH, m), BF16),
                   jax.ShapeDtypeStruct((b, m, MEM_WIDTH), BF16)],
        compiler_params=_cparams(("parallel",)),
        name="mem_kv",
    )(mem, wkt, wv)


def _mem_attn_kernel(ql_ref, qh_ref, kt_ref, v_ref, o_ref):
    cols = lambda m: slice(m * LANES, (m + 1) * LANES)
    zero = jnp.zeros((1, 1), F32)

    def store(m, out):
        o_ref[:, cols(m)] = out.astype(BF16)

    _paired_heads_attention(
        MEM_HEADS,
        q_of=lambda h: (ql_ref, qh_ref)[h % 2][:, cols(h // 2)],
        kt_of=lambda m: kt_ref[0, cols(m), :],
        v_of=lambda m: v_ref[0, :, cols(m)],
        bias_of=lambda h: zero,
        store=store)


def _mem_attention(ql, qh, kt, v, batch, seq):
    n = ql.shape[0]
    tm = TOKEN_TILE
    nt = seq // tm
    m = kt.shape[2]
    qspec = pl.BlockSpec((tm, MEM_WIDTH), lambda b, i: (b * nt + i, 0))
    return pl.pallas_call(
        _mem_attn_kernel,
        grid=(batch, nt),
        in_specs=[qspec, qspec,
                  pl.BlockSpec((1, MEM_WIDTH, m), lambda b, i: (b, 0, 0)),
                  pl.BlockSpec((1, m, MEM_WIDTH), lambda b, i: (b, 0, 0))],
        out_specs=qspec,
        out_shape=jax.ShapeDtypeStruct((n, MEM_WIDTH), BF16),
        compiler_params=_cparams(("parallel", "parallel")),
        name="mem_attn",
    )(ql, qh, kt, v)


def _out_proj_kernel(xn_ref, na_ref, pool_ref, mem_ref, w1_ref, w2_ref, w3_ref, g_ref, b_ref, o_ref):
    mixed = _dot(na_ref[...], w1_ref[...]) + _dot(pool_ref[...], w2_ref[...]) + _dot(mem_ref[...], w3_ref[...])
    o_ref[...] = _layer_norm(ALPHA * xn_ref[...] + mixed, g_ref[...], b_ref[...])


def _out_proj(xn, y_na, y_pool, y_mem, w1, w2, w3, g, b):
    n = xn.shape[0]
    tm = TOKEN_TILE
    row = lambda w: pl.BlockSpec((tm, w), lambda i: (i, 0))
    full = lambda a: pl.BlockSpec(a.shape, lambda i: (0,) * a.ndim)
    return pl.pallas_call(
        _out_proj_kernel,
        grid=(n // tm,),
        in_specs=[row(D_MODEL), row(NA_WIDTH), row(POOL_WIDTH), row(MEM_WIDTH),
                  full(w1), full(w2), full(w3), full(g), full(b)],
        out_specs=row(D_MODEL),
        out_shape=jax.ShapeDtypeStruct((n, D_MODEL), F32),
        compiler_params=_cparams(("parallel",)),
        name="out_proj",
    )(xn, y_na, y_pool, y_mem, w1, w2, w3, g, b)


def _staircase():
    return [(a, b) for a in range(PEER_TOPK) for b in range(PEER_TOPK) if (a + 1) * (b + 1) <= PEER_TOPK]


def _route_kernel(x_ref, wqt_ref, keys_ref, s1_ref, e1_ref, phi_ref, a_ref, sc, sv):
    xb = x_ref[...].astype(BF16)
    qt = _dot_nt(wqt_ref[...], xb).astype(BF16)
    for hp in range(2 * PEER_HEADS):
        sc[hp] = _dot(keys_ref[hp % 2], qt[hp * PEER_HALF:(hp + 1) * PEER_HALF, :])

    ninf = F32(-jnp.inf)
    for h in range(PEER_HEADS):
        for p in range(2):
            s = sc[2 * h + p]
            m = None
            for k in range(PEER_TOPK):
                cand = s if m is None else jnp.where(s < m, s, ninf)
                m = jnp.max(cand, axis=0, keepdims=True)
                sv[p, k, h:h + 1, :] = m

    sv0 = [sv[0, k] for k in range(PEER_TOPK)]
    sv1 = [sv[1, k] for k in range(PEER_TOPK)]
    stair = _staircase()
    cands = [sv0[a] + sv1[b] for a, b in stair]
    top = cands[0]
    tau = top
    for _ in range(PEER_TOPK - 1):
        nxt = None
        for c in cands[1:]:
            v = jnp.where(c < tau, c, ninf)
            nxt = v if nxt is None else jnp.maximum(nxt, v)
        tau = nxt
    z = None
    for c in cands:
        v = jnp.where(c >= tau, jnp.exp(c - top), 0.0)
        z = v if z is None else z + v
    half_inv_z = 0.5 / z
    pinf = F32(jnp.inf)
    floor = [None] * PEER_TOPK
    for (a, b), c in zip(stair, cands):
        v = jnp.where(c >= tau, sv1[b], pinf)
        floor[a] = v if floor[a] is None else jnp.minimum(floor[a], v)

    for h in range(PEER_HEADS):
        hs = slice(h, h + 1)
        s0 = sc[2 * h]
        s1 = sc[2 * h + 1]
        phi = jnp.full_like(s0, pinf)
        for a in range(PEER_TOPK):
            phi = jnp.where(s0 == sv0[a][hs], floor[a][hs], phi)
        phi_ref[h] = phi
        a_ref[h] = jnp.exp(s0 - sv0[0][hs]) * half_inv_z[hs]
        s1_ref[h] = s1
        e1_ref[h] = jnp.exp(s1 - sv1[0][hs])


def _route(x1, wqt, keys):
    n = x1.shape[0]
    tr = ROUTE_TILE
    ospec = pl.BlockSpec((PEER_HEADS, PEER_NKEYS, tr), lambda i: (0, 0, i))
    oshape = jax.ShapeDtypeStruct((PEER_HEADS, PEER_NKEYS, n), F32)
    return pl.pallas_call(
        _route_kernel,
        grid=(n // tr,),
        in_specs=[pl.BlockSpec((tr, D_MODEL), lambda i: (i, 0)),
                  pl.BlockSpec(wqt.shape, lambda i: (0, 0)),
                  pl.BlockSpec(keys.shape, lambda i: (0, 0, 0))],
        out_specs=[ospec] * 4,
        out_shape=[oshape] * 4,
        scratch_shapes=[pltpu.VMEM((2 * PEER_HEADS, PEER_NKEYS, tr), F32),
                        pltpu.VMEM((2, PEER_TOPK, PEER_HEADS, tr), F32)],
        compiler_params=_cparams(("parallel",)),
        name="peer_route",
    )(x1, wqt, keys)


_SUB_ROWS = 16
_TILE_BLOCKS = 8
_BLOCKS = PEER_CHUNK // PEER_NKEYS
_N_CHUNKS = PEER_N // PEER_CHUNK
_MM_ROWS = 512
_MM_COLS = 256


def _peer_kernel(x1_ref, x3_ref, u_ref, vt_ref, s1_ref, e1_ref, phi_ref, a_ref, g_ref, b_ref, o_ref,
                 xt, ht0, ht1, wt0, wt1, acc, *, n_work):
    s = pl.program_id(0)
    tm = x1_ref.shape[0]
    ht = (ht0, ht1)
    wt = (wt0, wt1)
    n_piece = (PEER_CHUNK // _MM_ROWS) * (tm // _MM_COLS)
    assert n_piece == tm // LANES and _N_CHUNKS % 2 == 0 and n_work % 2 == 0
    chunk1 = s % _N_CHUNKS
    chunk3 = (s + _N_CHUNKS - 2) % _N_CHUNKS
    xslot = (s // _N_CHUNKS) % 2

    def gate_tile(slot, c0, r0, blk0):
        gates = [jnp.zeros((_SUB_ROWS, LANES), F32) for _ in range(_TILE_BLOCKS)]
        for h in range(PEER_HEADS):
            s1 = s1_ref[h, pl.ds(r0, _SUB_ROWS), pl.ds(c0, LANES)]
            e1 = e1_ref[h, pl.ds(r0, _SUB_ROWS), pl.ds(c0, LANES)]
            for i in range(_TILE_BLOCKS):
                phi = phi_ref[h, blk0 + i:blk0 + i + 1, pl.ds(c0, LANES)]
                a = a_ref[h, blk0 + i:blk0 + i + 1, pl.ds(c0, LANES)]
                gates[i] = gates[i] + jnp.where(s1 >= phi, e1, 0.0) * a
        for i in range(_TILE_BLOCKS):
            rows = pl.ds((blk0 + i) * PEER_NKEYS + r0, _SUB_ROWS)
            hh = ht[slot][rows, pl.ds(c0, LANES)]
            act = hh * (1.0 + lax.erf(hh * INV_SQRT2))
            wt[slot][rows, pl.ds(c0, LANES)] = (gates[i] * act).astype(BF16)

    def step(p, stage1, stage2, stage3):
        def piece(it, carry):
            m0 = pl.multiple_of((it // (tm // _MM_COLS)) * _MM_ROWS, _MM_ROWS)
            n0 = pl.multiple_of((it % (tm // _MM_COLS)) * _MM_COLS, _MM_COLS)
            rows, cols = pl.ds(m0, _MM_ROWS), pl.ds(n0, _MM_COLS)
            packed_rows = pl.ds(pl.multiple_of(m0 // 2, _MM_ROWS // 2), _MM_ROWS // 2)
            if stage3:
                acc[rows, cols] += _dot(pltpu.bitcast(vt_ref[packed_rows, :], BF16), wt[p][:, cols])
            if stage1:
                ht[p][rows, cols] = _dot(pltpu.bitcast(u_ref[packed_rows, :], BF16), xt[xslot, :, cols])
            if stage2:
                c0 = pl.multiple_of(it * LANES, LANES)
                for r0 in range(0, PEER_NKEYS, _SUB_ROWS):
                    for blk0 in range(0, _BLOCKS, _TILE_BLOCKS):
                        gate_tile(1 - p, c0, r0, blk0)
            return carry

        if stage1:
            @pl.when(chunk1 == 0)
            def _():
                xt[xslot] = x1_ref[...].T.astype(BF16)

        if stage3:
            @pl.when(chunk3 == 0)
            def _():
                acc[...] = jnp.zeros_like(acc)

        lax.fori_loop(0, n_piece, piece, 0)

        if stage3:
            @pl.when(chunk3 == _N_CHUNKS - 1)
            def _():
                o_ref[...] = _layer_norm(ALPHA * x3_ref[...] + acc[...].T, g_ref[...], b_ref[...])

    @pl.when(s == 0)
    def _():
        step(0, True, False, False)

    @pl.when(s == 1)
    def _():
        step(1, True, True, False)

    for p in range(2):
        @pl.when((s >= 2) & (s < n_work) & (s % 2 == p))
        def _(p=p):
            step(p, True, True, True)

    @pl.when(s == n_work)
    def _():
        step(0, False, True, True)

    @pl.when(s == n_work + 1)
    def _():
        step(1, False, False, True)


def _peer(x1, u, vt, s1, e1, phi, a, g, b):
    n = x1.shape[0]
    tm = PEER_TILE
    ec = PEER_CHUNK
    n_tiles = n // tm
    n_work = n_tiles * _N_CHUNKS
    tile = lambda j: jnp.clip(j // _N_CHUNKS, 0, n_tiles - 1)
    chunk = lambda j: (j + _N_CHUNKS) % _N_CHUNKS
    xspec = lambda lag: pl.BlockSpec((tm, D_MODEL), lambda j: (tile(j - lag), 0))
    dense = pl.BlockSpec((PEER_HEADS, PEER_NKEYS, tm), lambda j: (0, 0, tile(j - 1)))
    per_blk = pl.BlockSpec((PEER_HEADS, _BLOCKS, tm), lambda j: (0, chunk(j - 1), tile(j - 1)))
    vec = pl.BlockSpec((1, D_MODEL), lambda j: (0, 0))
    return pl.pallas_call(
        functools.partial(_peer_kernel, n_work=n_work),
        grid=(n_work + 2,),
        in_specs=[xspec(0), xspec(2),
                  pl.BlockSpec((ec // 2, D_MODEL), lambda j: (chunk(j), 0)),
                  pl.BlockSpec((D_MODEL // 2, ec), lambda j: (0, chunk(j - 2))),
                  dense, dense, per_blk, per_blk, vec, vec],
        out_specs=xspec(2),
        out_shape=jax.ShapeDtypeStruct((n, D_MODEL), F32),
        scratch_shapes=[pltpu.VMEM((2, D_MODEL, tm), BF16),
                        pltpu.VMEM((ec, tm), F32), pltpu.VMEM((ec, tm), F32),
                        pltpu.VMEM((ec, tm), BF16), pltpu.VMEM((ec, tm), BF16),
                        pltpu.VMEM((D_MODEL, tm), F32)],
        compiler_params=_cparams(("arbitrary",)),
        name="peer_dense",
    )(x1, x1, u, vt, s1, e1, phi, a, g, b)


def _pack_row_pairs(x):
    r, c = x.shape
    return lax.bitcast_convert_type(x.reshape(r // 2, 2, c).transpose(0, 2, 1), jnp.uint32)


def _head_pair_masks(width):
    head = np.arange(width) // HEAD_DIM
    lo = (head % 2 == 0).astype(np.float32)
    return lo, 1.0 - lo


def _prepare(ln_in_g, ln_in_b, w_in, w_mem_kv, na_rpb, w_pool, pool_scale, w_out, ln1_g, ln1_b,
             w_query, sub_keys, peer_u, peer_v, ln2_g, ln2_b):
    wi = w_in[0]
    o = NA_WIDTH
    wq, wk, wv = wi[:, 0:o], wi[:, o:2 * o], wi[:, 2 * o:3 * o]
    wp = wi[:, 3 * o:3 * o + POOL_WIDTH]
    wm = wi[:, 3 * o + POOL_WIDTH:]
    lo, hi = _head_pair_masks(NA_WIDTH)
    mlo, mhi = _head_pair_masks(MEM_WIDTH)
    w_bd = jnp.zeros((POOL_WIDTH, POOL_WIDTH), F32)
    for g in range(len(POOL_WINDOWS)):
        sl = slice(g * POOL_GROUP, (g + 1) * POOL_GROUP)
        w_bd = w_bd.at[sl, sl].set(w_pool[0, g])
    wkv = w_mem_kv[0]
    wo = w_out[0].astype(BF16)
    row = lambda v: v.reshape(1, -1).astype(F32)
    return dict(
        ln_in_g=row(ln_in_g), ln_in_b=row(ln_in_b),
        wql=(wq * lo).astype(BF16), wqh=(wq * hi).astype(BF16), wkt=wk.T.astype(BF16), wv=wv.astype(BF16),
        wp=wp.astype(BF16), wml=(wm * mlo).astype(BF16), wmh=(wm * mhi).astype(BF16),
        mem_wkt=wkv[:, :MEM_WIDTH].T.astype(BF16), mem_wv=wkv[:, MEM_WIDTH:].astype(BF16),
        na_bias=_na_bias_table(na_rpb[0]),
        w_bd=w_bd.astype(BF16), pool_scale=row(pool_scale[0]),
        wo_na=wo[:NA_WIDTH], wo_pool=wo[NA_WIDTH:NA_WIDTH + POOL_WIDTH], wo_mem=wo[NA_WIDTH + POOL_WIDTH:],
        ln1_g=row(ln1_g[0]), ln1_b=row(ln1_b[0]),
        wqt=w_query[0].T.astype(BF16), keys=sub_keys[0].astype(BF16),
        u=_pack_row_pairs(peer_u[0].astype(BF16)), vt=_pack_row_pairs(peer_v[0].T.astype(BF16)),
        ln2_g=row(ln2_g[0]), ln2_b=row(ln2_b[0]),
    )


def _trunk(x, mem, w):
    batch, seq, _ = x.shape
    rows = seq // GRID_W
    xf = x.reshape(batch * seq, D_MODEL)
    xn, ql, qh, kt, v, xp, ml, mh = _in_proj(xf, w["ln_in_g"], w["ln_in_b"], w["wql"], w["wqh"], w["wkt"],
                                             w["wv"], w["wp"], w["wml"], w["wmh"])
    y_na = _na_attention(ql, qh, kt, v, w["na_bias"], batch, rows)
    y_pool = _pool(xp, w["w_bd"], w["pool_scale"], batch, seq)
    mkt, mv = _mem_kv(mem, w["mem_wkt"], w["mem_wv"])
    y_mem = _mem_attention(ml, mh, mkt, mv, batch, seq)
    x1 = _out_proj(xn, y_na, y_pool, y_mem, w["wo_na"], w["wo_pool"], w["wo_mem"], w["ln1_g"], w["ln1_b"])
    s1, e1, phi, a = _route(x1, w["wqt"], w["keys"])
    y = _peer(x1, w["u"], w["vt"], s1, e1, phi, a, w["ln2_g"], w["ln2_b"])
    return y.reshape(batch, seq, D_MODEL)


def kernel(x_prompt, x_sample, mem_prompt, mem_sample, ln_in_g, ln_in_b, w_in, w_mem_kv, na_rpb, w_pool, pool_scale, w_out, ln1_g, ln1_b, w_query, sub_keys, peer_u, peer_v, ln2_g, ln2_b):
    w = _prepare(ln_in_g, ln_in_b, w_in, w_mem_kv, na_rpb, w_pool, pool_scale, w_out, ln1_g, ln1_b,
                 w_query, sub_keys, peer_u, peer_v, ln2_g, ln2_b)
    return (_trunk(x_prompt, mem_prompt, w), _trunk(x_sample, mem_sample, w))
```

```python
import functools

import numpy as np
import jax
import jax.numpy as jnp
from jax import lax
from jax.experimental import pallas as pl
from jax.experimental.pallas import tpu as pltpu

F32 = jnp.float32
BF16 = jnp.bfloat16

D_MODEL = 1024
GRID_W = 64
HEAD_DIM = 64
NA_HEADS = 8
NA_WIDTH = NA_HEADS * HEAD_DIM
NA_ROWS = 8
NA_COLS = 16
POOL_WINDOWS = (2, 4, 8, 16)
POOL_GROUP = 64
POOL_WIDTH = POOL_GROUP * len(POOL_WINDOWS)
MEM_HEADS = 4
MEM_WIDTH = MEM_HEADS * HEAD_DIM
PEER_HEADS = 8
PEER_NKEYS = 128
PEER_N = PEER_NKEYS * PEER_NKEYS
PEER_HALF = 64
PEER_TOPK = 16
DEPTH = 1
ALPHA = (2.0 * DEPTH) ** 0.25
LN_EPS = 1e-5
QK_SCALE = HEAD_DIM ** -0.5
NEG = -1e30
INV_SQRT2 = 0.7071067811865476

LANES = 128
SUPER_ROW = 2 * GRID_W
NA_WIN = 5
NA_BIAS_SPAN = 9
VMEM_LIMIT = 56 * 1024 * 1024

TOKEN_TILE = 512
ROUTE_TILE = 256
PEER_TILE = 512
PEER_CHUNK = 2048


def _cparams(sem):
    return pltpu.CompilerParams(dimension_semantics=sem, vmem_limit_bytes=VMEM_LIMIT)


def _layer_norm(x, g, b):
    mu = jnp.mean(x, axis=-1, keepdims=True)
    xc = x - mu
    var = jnp.mean(xc * xc, axis=-1, keepdims=True)
    return xc * lax.rsqrt(var + LN_EPS) * g + b


def _dot(a, b):
    return jnp.dot(a, b, preferred_element_type=F32)


def _dot_nt(a, b):
    return lax.dot_general(a, b, (((1,), (1,)), ((), ())), preferred_element_type=F32)


def _in_proj_kernel(x_ref, g_ref, b_ref, wql_ref, wqh_ref, wkt_ref, wv_ref, wp_ref, wml_ref, wmh_ref,
                    xn_ref, ql_ref, qh_ref, kt_ref, v_ref, xp_ref, ml_ref, mh_ref):
    y = _layer_norm(x_ref[...], g_ref[...], b_ref[...])
    xn_ref[...] = y
    yb = y.astype(BF16)
    ql_ref[...] = (_dot(yb, wql_ref[...]) * QK_SCALE).astype(BF16)
    qh_ref[...] = (_dot(yb, wqh_ref[...]) * QK_SCALE).astype(BF16)
    kt_ref[...] = _dot_nt(wkt_ref[...], yb).astype(BF16)
    v_ref[...] = _dot(yb, wv_ref[...]).astype(BF16)
    xp_ref[...] = _dot(yb, wp_ref[...])
    ml_ref[...] = (_dot(yb, wml_ref[...]) * QK_SCALE).astype(BF16)
    mh_ref[...] = (_dot(yb, wmh_ref[...]) * QK_SCALE).astype(BF16)


def _in_proj(x, g, b, wql, wqh, wkt, wv, wp, wml, wmh):
    n = x.shape[0]
    tm = TOKEN_TILE
    row = lambda w: pl.BlockSpec((tm, w), lambda i: (i, 0))
    full = lambda a: pl.BlockSpec(a.shape, lambda i: (0,) * a.ndim)
    return pl.pallas_call(
        _in_proj_kernel,
        grid=(n // tm,),
        in_specs=[row(D_MODEL), full(g), full(b), full(wql), full(wqh), full(wkt), full(wv), full(wp),
                  full(wml), full(wmh)],
        out_specs=[row(D_MODEL), row(NA_WIDTH), row(NA_WIDTH),
                   pl.BlockSpec((NA_WIDTH, tm), lambda i: (0, i)),
                   row(NA_WIDTH), row(POOL_WIDTH), row(MEM_WIDTH), row(MEM_WIDTH)],
        out_shape=[jax.ShapeDtypeStruct((n, D_MODEL), F32),
                   jax.ShapeDtypeStruct((n, NA_WIDTH), BF16),
                   jax.ShapeDtypeStruct((n, NA_WIDTH), BF16),
                   jax.ShapeDtypeStruct((NA_WIDTH, n), BF16),
                   jax.ShapeDtypeStruct((n, NA_WIDTH), BF16),
                   jax.ShapeDtypeStruct((n, POOL_WIDTH), F32),
                   jax.ShapeDtypeStruct((n, MEM_WIDTH), BF16),
                   jax.ShapeDtypeStruct((n, MEM_WIDTH), BF16)],
        compiler_params=_cparams(("parallel",)),
        name="in_proj",
    )(x, g, b, wql, wqh, wkt, wv, wp, wml, wmh)


_ATTN_LOOKAHEAD = 2


def _paired_heads_attention(n_heads, q_of, kt_of, v_of, bias_of, store):
    def scores(h):
        return _dot(q_of(h), kt_of(h // 2)) + bias_of(h)

    pending = [scores(h) for h in range(min(_ATTN_LOOKAHEAD, n_heads))]
    outs = []
    for h in range(n_heads):
        s = pending.pop(0)
        if h + _ATTN_LOOKAHEAD < n_heads:
            pending.append(scores(h + _ATTN_LOOKAHEAD))
        mx = jnp.max(s, axis=-1, keepdims=True)
        p = jnp.exp(s - mx)
        l = jnp.sum(p, axis=-1, keepdims=True)
        outs.append(_dot(p.astype(BF16), v_of(h // 2)) * (1.0 / l))
        if h % 2 == 1:
            lane = lax.broadcasted_iota(jnp.int32, outs[0].shape, 1)
            store(h // 2, jnp.where(lane < HEAD_DIM, outs[0], outs[1]))
            outs = []


def _na_kernel(ql_ref, qh_ref, *refs, n_sr, rows):
    kt_refs = refs[0:NA_WIN]
    v_refs = refs[NA_WIN:2 * NA_WIN]
    bias_ref = refs[2 * NA_WIN]
    o_ref = refs[2 * NA_WIN + 1]
    kwin, vwin = refs[2 * NA_WIN + 2:]
    r = pl.program_id(1)
    sr0 = jnp.clip(r - 2, 0, n_sr - NA_WIN)
    for s in range(NA_WIN):
        kwin[:, s * SUPER_ROW:(s + 1) * SUPER_ROW] = kt_refs[s][...]
        vwin[s * SUPER_ROW:(s + 1) * SUPER_ROW, :] = v_refs[s][...]
    nk = NA_WIN * SUPER_ROW
    qi = lax.broadcasted_iota(jnp.int32, (SUPER_ROW, nk), 0)
    ki = lax.broadcasted_iota(jnp.int32, (SUPER_ROW, nk), 1)
    qrow = 2 * r + jnp.where(qi >= GRID_W, 1, 0)
    krow = 2 * sr0 + lax.shift_right_logical(ki, 6)
    rs = jnp.clip(qrow - NA_ROWS // 2, 0, rows - NA_ROWS)
    rmask = jnp.where((krow >= rs) & (krow < rs + NA_ROWS), 0.0, NEG).astype(F32)
    boff = pl.multiple_of((sr0 - r + NA_BIAS_SPAN // 2) * SUPER_ROW, SUPER_ROW)
    cols = lambda m: slice(m * LANES, (m + 1) * LANES)

    def store(m, out):
        o_ref[:, cols(m)] = out.astype(BF16)

    _paired_heads_attention(
        NA_HEADS,
        q_of=lambda h: (ql_ref, qh_ref)[h % 2][:, cols(h // 2)],
        kt_of=lambda m: kwin[cols(m), :],
        v_of=lambda m: vwin[:, cols(m)],
        bias_of=lambda h: bias_ref[h, :, pl.ds(boff, nk)] + rmask,
        store=store)


def _na_attention(ql, qh, kt, v, bias, batch, rows):
    n = ql.shape[0]
    n_sr = rows // 2
    assert rows >= 2 * NA_WIN and rows % 2 == 0 and n == batch * n_sr * SUPER_ROW
    qspec = pl.BlockSpec((SUPER_ROW, NA_WIDTH), lambda b, r: (b * n_sr + r, 0))

    def win(r):
        return jnp.clip(r - 2, 0, n_sr - NA_WIN)

    kt_specs = [pl.BlockSpec((NA_WIDTH, SUPER_ROW), lambda b, r, s=s: (0, b * n_sr + win(r) + s))
                for s in range(NA_WIN)]
    v_specs = [pl.BlockSpec((SUPER_ROW, NA_WIDTH), lambda b, r, s=s: (b * n_sr + win(r) + s, 0))
               for s in range(NA_WIN)]
    return pl.pallas_call(
        functools.partial(_na_kernel, n_sr=n_sr, rows=rows),
        grid=(batch, n_sr),
        in_specs=[qspec, qspec] + kt_specs + v_specs + [pl.BlockSpec(bias.shape, lambda b, r: (0, 0, 0))],
        out_specs=qspec,
        out_shape=jax.ShapeDtypeStruct((n, NA_WIDTH), BF16),
        scratch_shapes=[pltpu.VMEM((NA_WIDTH, NA_WIN * SUPER_ROW), BF16),
                        pltpu.VMEM((NA_WIN * SUPER_ROW, NA_WIDTH), BF16)],
        compiler_params=_cparams(("parallel", "parallel")),
        name="na_attn",
    )(ql, qh, *([kt] * NA_WIN), *([v] * NA_WIN), bias)


def _na_bias_table(rpb):
    h = rpb.shape[0]
    pad = GRID_W - NA_COLS
    padded = jnp.pad(rpb.astype(F32), ((0, 0), (0, 0), (pad, pad)))
    colexp = jnp.stack([padded[:, :, pad + NA_COLS - 1 - qc:pad + NA_COLS - 1 - qc + GRID_W]
                        for qc in range(GRID_W)], axis=2)
    zeros = jnp.zeros((h, GRID_W, GRID_W), F32)
    per_parity = []
    for qp in range(2):
        slots = []
        for dsr in range(-(NA_BIAS_SPAN // 2), NA_BIAS_SPAN // 2 + 1):
            for kp in range(2):
                r = 2 * dsr + kp - qp + NA_ROWS - 1
                slots.append(colexp[:, r] if 0 <= r <= 2 * NA_ROWS - 2 else zeros)
        per_parity.append(jnp.concatenate(slots, axis=2))
    tab = jnp.concatenate(per_parity, axis=1)
    qc = np.arange(SUPER_ROW) % GRID_W
    kc = np.arange(NA_BIAS_SPAN * SUPER_ROW) % GRID_W
    wstart = np.clip(qc - NA_COLS // 2, 0, GRID_W - NA_COLS)
    col_ok = (kc[None, :] >= wstart[:, None]) & (kc[None, :] < wstart[:, None] + NA_COLS)
    return jnp.where(col_ok[None], tab, NEG)


def _pool_kernel(prev_ref, cur_ref, next_ref, w_ref, sc_ref, o_ref, buf, *, seq, tm):
    i = pl.program_id(1)
    nt = seq // tm
    halo = 8
    zero = jnp.zeros((halo, POOL_WIDTH), F32)
    buf[0:halo, :] = jnp.where(i == 0, zero, prev_ref[...])
    buf[halo:halo + tm, :] = cur_ref[...]
    buf[halo + tm:2 * halo + tm, :] = jnp.where(i == nt - 1, zero, next_ref[...])
    pos = i * tm + lax.broadcasted_iota(jnp.int32, (tm, LANES), 0)
    lane = lax.broadcasted_iota(jnp.int32, (tm, LANES), 1)

    def shifted(d, c):
        return buf[halo + d:halo + d + tm, c * LANES:(c + 1) * LANES]

    def count(w):
        return (jnp.minimum(pos + w // 2, seq) - jnp.maximum(pos - w // 2, 0)).astype(F32)

    x0 = shifted(0, 0)
    s2 = shifted(-1, 0) + x0
    s4 = s2 + shifted(-2, 0) + shifted(1, 0)
    pa = jnp.where(lane < POOL_GROUP, s2 / count(2), s4 / count(4)) - x0
    x1 = shifted(0, 1)
    s8 = x1
    for d in (-4, -3, -2, -1, 1, 2, 3):
        s8 = s8 + shifted(d, 1)
    s16 = s8
    for d in (-8, -7, -6, -5, 4, 5, 6, 7):
        s16 = s16 + shifted(d, 1)
    pb = jnp.where(lane < POOL_GROUP, s8 / count(8), s16 / count(16)) - x1
    pooled = jnp.concatenate([pa, pb], axis=1).astype(BF16)
    o_ref[...] = (_dot(pooled, w_ref[...]) * sc_ref[...]).astype(BF16)


def _pool(xp, w_bd, scale, batch, seq):
    n = xp.shape[0]
    tm = TOKEN_TILE
    nt = seq // tm
    assert seq % tm == 0 and n == batch * seq
    return pl.pallas_call(
        functools.partial(_pool_kernel, seq=seq, tm=tm),
        grid=(batch, nt),
        in_specs=[
            pl.BlockSpec((8, POOL_WIDTH), lambda b, i: (jnp.maximum((b * seq + i * tm) // 8 - 1, 0), 0)),
            pl.BlockSpec((tm, POOL_WIDTH), lambda b, i: (b * nt + i, 0)),
            pl.BlockSpec((8, POOL_WIDTH),
                         lambda b, i: (jnp.minimum((b * seq + (i + 1) * tm) // 8, n // 8 - 1), 0)),
            pl.BlockSpec(w_bd.shape, lambda b, i: (0, 0)),
            pl.BlockSpec(scale.shape, lambda b, i: (0, 0)),
        ],
        out_specs=pl.BlockSpec((tm, POOL_WIDTH), lambda b, i: (b * nt + i, 0)),
        out_shape=jax.ShapeDtypeStruct((n, POOL_WIDTH), BF16),
        scratch_shapes=[pltpu.VMEM((tm + 16, POOL_WIDTH), F32)],
        compiler_params=_cparams(("parallel", "parallel")),
        name="pool",
    )(xp, xp, xp, w_bd, scale)


def _mem_kv_kernel(mem_ref, wkt_ref, wv_ref, kt_ref, v_ref):
    mb = mem_ref[0].astype(BF16)
    kt_ref[0] = _dot_nt(wkt_ref[...], mb).astype(BF16)
    v_ref[0] = _dot(mb, wv_ref[...]).astype(BF16)


def _mem_kv(mem, wkt, wv):
    b, m, _ = mem.shape
    return pl.pallas_call(
        _mem_kv_kernel,
        grid=(b,),
        in_specs=[pl.BlockSpec((1, m, D_MODEL), lambda i: (i, 0, 0)),
                  pl.BlockSpec(wkt.shape, lambda i: (0, 0)),
                  pl.BlockSpec(wv.shape, lambda i: (0, 0))],
        out_specs=[pl.BlockSpec((1, MEM_WIDTH, m), lambda i: (i, 0, 0)),
                   pl.BlockSpec((1, m, MEM_WIDTH), lambda i: (i, 0, 0))],
        out_shape=[jax.ShapeDtypeStruct((b, MEM_WIDTH, m), BF16),
                   jax.ShapeDtypeStruct((b, m, MEM_WIDTH), BF16)],
        compiler_params=_cparams(("parallel",)),
        name="mem_kv",
    )(mem, wkt, wv)


def _mem_attn_kernel(ql_ref, qh_ref, kt_ref, v_ref, o_ref):
    cols = lambda m: slice(m * LANES, (m + 1) * LANES)
    zero = jnp.zeros((1, 1), F32)

    def store(m, out):
        o_ref[:, cols(m)] = out.astype(BF16)

    _paired_heads_attention(
        MEM_HEADS,
        q_of=lambda h: (ql_ref, qh_ref)[h % 2][:, cols(h // 2)],
        kt_of=lambda m: kt_ref[0, cols(m), :],
        v_of=lambda m: v_ref[0, :, cols(m)],
        bias_of=lambda h: zero,
        store=store)


def _mem_attention(ql, qh, kt, v, batch, seq):
    n = ql.shape[0]
    tm = TOKEN_TILE
    nt = seq // tm
    m = kt.shape[2]
    qspec = pl.BlockSpec((tm, MEM_WIDTH), lambda b, i: (b * nt + i, 0))
    return pl.pallas_call(
        _mem_attn_kernel,
        grid=(batch, nt),
        in_specs=[qspec, qspec,
                  pl.BlockSpec((1, MEM_WIDTH, m), lambda b, i: (b, 0, 0)),
                  pl.BlockSpec((1, m, MEM_WIDTH), lambda b, i: (b, 0, 0))],
        out_specs=qspec,
        out_shape=jax.ShapeDtypeStruct((n, MEM_WIDTH), BF16),
        compiler_params=_cparams(("parallel", "parallel")),
        name="mem_attn",
    )(ql, qh, kt, v)


def _out_proj_kernel(xn_ref, na_ref, pool_ref, mem_ref, w1_ref, w2_ref, w3_ref, g_ref, b_ref, o_ref):
    mixed = _dot(na_ref[...], w1_ref[...]) + _dot(pool_ref[...], w2_ref[...]) + _dot(mem_ref[...], w3_ref[...])
    o_ref[...] = _layer_norm(ALPHA * xn_ref[...] + mixed, g_ref[...], b_ref[...])


def _out_proj(xn, y_na, y_pool, y_mem, w1, w2, w3, g, b):
    n = xn.shape[0]
    tm = TOKEN_TILE
    row = lambda w: pl.BlockSpec((tm, w), lambda i: (i, 0))
    full = lambda a: pl.BlockSpec(a.shape, lambda i: (0,) * a.ndim)
    return pl.pallas_call(
        _out_proj_kernel,
        grid=(n // tm,),
        in_specs=[row(D_MODEL), row(NA_WIDTH), row(POOL_WIDTH), row(MEM_WIDTH),
                  full(w1), full(w2), full(w3), full(g), full(b)],
        out_specs=row(D_MODEL),
        out_shape=jax.ShapeDtypeStruct((n, D_MODEL), F32),
        compiler_params=_cparams(("parallel",)),
        name="out_proj",
    )(xn, y_na, y_pool, y_mem, w1, w2, w3, g, b)


def _staircase():
    return [(a, b) for a in range(PEER_TOPK) for b in range(PEER_TOPK) if (a + 1) * (b + 1) <= PEER_TOPK]


def _route_kernel(x_ref, wqt_ref, keys_ref, s1_ref, e1_ref, phi_ref, a_ref, sc, sv):
    xb = x_ref[...].astype(BF16)
    qt = _dot_nt(wqt_ref[...], xb).astype(BF16)
    for hp in range(2 * PEER_HEADS):
        sc[hp] = _dot(keys_ref[hp % 2], qt[hp * PEER_HALF:(hp + 1) * PEER_HALF, :])

    ninf = F32(-jnp.inf)
    for h in range(PEER_HEADS):
        for p in range(2):
            s = sc[2 * h + p]
            m = None
            for k in range(PEER_TOPK):
                cand = s if m is None else jnp.where(s < m, s, ninf)
                m = jnp.max(cand, axis=0, keepdims=True)
                sv[p, k, h:h + 1, :] = m

    sv0 = [sv[0, k] for k in range(PEER_TOPK)]
    sv1 = [sv[1, k] for k in range(PEER_TOPK)]
    stair = _staircase()
    cands = [sv0[a] + sv1[b] for a, b in stair]
    top = cands[0]
    tau = top
    for _ in range(PEER_TOPK - 1):
        nxt = None
        for c in cands[1:]:
            v = jnp.where(c < tau, c, ninf)
            nxt = v if nxt is None else jnp.maximum(nxt, v)
        tau = nxt
    z = None
    for c in cands:
        v = jnp.where(c >= tau, jnp.exp(c - top), 0.0)
        z = v if z is None else z + v
    half_inv_z = 0.5 / z
    pinf = F32(jnp.inf)
    floor = [None] * PEER_TOPK
    for (a, b), c in zip(stair, cands):
        v = jnp.where(c >= tau, sv1[b], pinf)
        floor[a] = v if floor[a] is None else jnp.minimum(floor[a], v)

    for h in range(PEER_HEADS):
        hs = slice(h, h + 1)
        s0 = sc[2 * h]
        s1 = sc[2 * h + 1]
        phi = jnp.full_like(s0, pinf)
        for a in range(PEER_TOPK):
            phi = jnp.where(s0 == sv0[a][hs], floor[a][hs], phi)
        phi_ref[h] = phi
        a_ref[h] = jnp.exp(s0 - sv0[0][hs]) * half_inv_z[hs]
        s1_ref[h] = s1
        e1_ref[h] = jnp.exp(s1 - sv1[0][hs])


def _route(x1, wqt, keys):
    n = x1.shape[0]
    tr = ROUTE_TILE
    ospec = pl.BlockSpec((PEER_HEADS, PEER_NKEYS, tr), lambda i: (0, 0, i))
    oshape = jax.ShapeDtypeStruct((PEER_HEADS, PEER_NKEYS, n), F32)
    return pl.pallas_call(
        _route_kernel,
        grid=(n // tr,),
        in_specs=[pl.BlockSpec((tr, D_MODEL), lambda i: (i, 0)),
                  pl.BlockSpec(wqt.shape, lambda i: (0, 0)),
                  pl.BlockSpec(keys.shape, lambda i: (0, 0, 0))],
        out_specs=[ospec] * 4,
        out_shape=[oshape] * 4,
        scratch_shapes=[pltpu.VMEM((2 * PEER_HEADS, PEER_NKEYS, tr), F32),
                        pltpu.VMEM((2, PEER_TOPK, PEER_HEADS, tr), F32)],
        compiler_params=_cparams(("parallel",)),
        name="peer_route",
    )(x1, wqt, keys)


_SUB_ROWS = 16
_TILE_BLOCKS = 8
_BLOCKS = PEER_CHUNK // PEER_NKEYS
_N_CHUNKS = PEER_N // PEER_CHUNK
_MM_ROWS = 512
_MM_COLS = 256


def _peer_kernel(x1_ref, x3_ref, u_ref, vt_ref, s1_ref, e1_ref, phi_ref, a_ref, g_ref, b_ref, o_ref,
                 xt, ht0, ht1, wt0, wt1, acc, *, n_work):
    s = pl.program_id(0)
    tm = x1_ref.shape[0]
    ht = (ht0, ht1)
    wt = (wt0, wt1)
    n_piece = tm // LANES
    col_blocks = tm // _MM_COLS
    mm1_per_iter = (PEER_CHUNK // _MM_ROWS) * col_blocks // n_piece
    mm3_per_iter = (D_MODEL // _MM_ROWS) * col_blocks // n_piece
    assert mm1_per_iter >= 1 and mm3_per_iter >= 1 and _N_CHUNKS % 2 == 0 and n_work % 2 == 0
    chunk1 = s % _N_CHUNKS
    chunk3 = (s + _N_CHUNKS - 2) % _N_CHUNKS
    xslot = (s // _N_CHUNKS) % 2

    def gate_tile(slot, c0, r0, blk0):
        gates = [jnp.zeros((_SUB_ROWS, LANES), F32) for _ in range(_TILE_BLOCKS)]
        for h in range(PEER_HEADS):
            s1 = s1_ref[h, pl.ds(r0, _SUB_ROWS), pl.ds(c0, LANES)]
            e1 = e1_ref[h, pl.ds(r0, _SUB_ROWS), pl.ds(c0, LANES)]
            for i in range(_TILE_BLOCKS):
                phi = phi_ref[h, blk0 + i:blk0 + i + 1, pl.ds(c0, LANES)]
                a = a_ref[h, blk0 + i:blk0 + i + 1, pl.ds(c0, LANES)]
                gates[i] = gates[i] + jnp.where(s1 >= phi, e1, 0.0) * a
        for i in range(_TILE_BLOCKS):
            rows = pl.ds((blk0 + i) * PEER_NKEYS + r0, _SUB_ROWS)
            hh = ht[slot][rows, pl.ds(c0, LANES)]
            act = hh * (1.0 + lax.erf(hh * INV_SQRT2))
            wt[slot][rows, pl.ds(c0, LANES)] = (gates[i] * act).astype(BF16)

    def step(p, stage1, stage2, stage3):
        def block(q):
            m0 = pl.multiple_of((q // col_blocks) * _MM_ROWS, _MM_ROWS)
            n0 = pl.multiple_of((q % col_blocks) * _MM_COLS, _MM_COLS)
            packed = pl.ds(pl.multiple_of(m0 // 2, _MM_ROWS // 2), _MM_ROWS // 2)
            return pl.ds(m0, _MM_ROWS), pl.ds(n0, _MM_COLS), packed

        def piece(it, carry):
            if stage3:
                for sub in range(mm3_per_iter):
                    rows, cols, packed = block(it * mm3_per_iter + sub)
                    acc[rows, cols] += _dot(pltpu.bitcast(vt_ref[packed, :], BF16), wt[p][:, cols])
            if stage1:
                for sub in range(mm1_per_iter):
                    rows, cols, packed = block(it * mm1_per_iter + sub)
                    ht[p][rows, cols] = _dot(pltpu.bitcast(u_ref[packed, :], BF16), xt[xslot, :, cols])
            if stage2:
                c0 = pl.multiple_of(it * LANES, LANES)
                for r0 in range(0, PEER_NKEYS, _SUB_ROWS):
                    for blk0 in range(0, _BLOCKS, _TILE_BLOCKS):
                        gate_tile(1 - p, c0, r0, blk0)
            return carry

        if stage1:
            @pl.when(chunk1 == 0)
            def _():
                xt[xslot] = x1_ref[...].T.astype(BF16)

        if stage3:
            @pl.when(chunk3 == 0)
            def _():
                acc[...] = jnp.zeros_like(acc)

        lax.fori_loop(0, n_piece, piece, 0)

        if stage3:
            @pl.when(chunk3 == _N_CHUNKS - 1)
            def _():
                o_ref[...] = _layer_norm(ALPHA * x3_ref[...] + acc[...].T, g_ref[...], b_ref[...])

    @pl.when(s == 0)
    def _():
        step(0, True, False, False)

    @pl.when(s == 1)
    def _():
        step(1, True, True, False)

    for p in range(2):
        @pl.when((s >= 2) & (s < n_work) & (s % 2 == p))
        def _(p=p):
            step(p, True, True, True)

    @pl.when(s == n_work)
    def _():
        step(0, False, True, True)

    @pl.when(s == n_work + 1)
    def _():
        step(1, False, False, True)


def _peer(x1, u, vt, s1, e1, phi, a, g, b):
    n = x1.shape[0]
    tm = PEER_TILE
    ec = PEER_CHUNK
    n_tiles = n // tm
    n_work = n_tiles * _N_CHUNKS
    tile = lambda j: jnp.clip(j // _N_CHUNKS, 0, n_tiles - 1)
    chunk = lambda j: (j + _N_CHUNKS) % _N_CHUNKS
    xspec = lambda lag: pl.BlockSpec((tm, D_MODEL), lambda j: (tile(j - lag), 0))
    dense = pl.BlockSpec((PEER_HEADS, PEER_NKEYS, tm), lambda j: (0, 0, tile(j - 1)))
    per_blk = pl.BlockSpec((PEER_HEADS, _BLOCKS, tm), lambda j: (0, chunk(j - 1), tile(j - 1)))
    vec = pl.BlockSpec((1, D_MODEL), lambda j: (0, 0))
    return pl.pallas_call(
        functools.partial(_peer_kernel, n_work=n_work),
        grid=(n_work + 2,),
        in_specs=[xspec(0), xspec(2),
                  pl.BlockSpec((ec // 2, D_MODEL), lambda j: (chunk(j), 0)),
                  pl.BlockSpec((D_MODEL // 2, ec), lambda j: (0, chunk(j - 2))),
                  dense, dense, per_blk, per_blk, vec, vec],
        out_specs=xspec(2),
        out_shape=jax.ShapeDtypeStruct((n, D_MODEL), F32),
        scratch_shapes=[pltpu.VMEM((2, D_MODEL, tm), BF16),
                        pltpu.VMEM((ec, tm), F32), pltpu.VMEM((ec, tm), F32),
                        pltpu.VMEM((ec, tm), BF16), pltpu.VMEM((ec, tm), BF16),
                        pltpu.VMEM((D_MODEL, tm), F32)],
        compiler_params=_cparams(("arbitrary",)),
        name="peer_dense",
    )(x1, x1, u, vt, s1, e1, phi, a, g, b)


def _pack_row_pairs(x):
    r, c = x.shape
    return lax.bitcast_convert_type(x.reshape(r // 2, 2, c).transpose(0, 2, 1), jnp.uint32)


def _head_pair_masks(width):
    head = np.arange(width) // HEAD_DIM
    lo = (head % 2 == 0).astype(np.float32)
    return lo, 1.0 - lo


def _prepare(ln_in_g, ln_in_b, w_in, w_mem_kv, na_rpb, w_pool, pool_scale, w_out, ln1_g, ln1_b,
             w_query, sub_keys, peer_u, peer_v, ln2_g, ln2_b):
    wi = w_in[0]
    o = NA_WIDTH
    wq, wk, wv = wi[:, 0:o], wi[:, o:2 * o], wi[:, 2 * o:3 * o]
    wp = wi[:, 3 * o:3 * o + POOL_WIDTH]
    wm = wi[:, 3 * o + POOL_WIDTH:]
    lo, hi = _head_pair_masks(NA_WIDTH)
    mlo, mhi = _head_pair_masks(MEM_WIDTH)
    w_bd = jnp.zeros((POOL_WIDTH, POOL_WIDTH), F32)
    for g in range(len(POOL_WINDOWS)):
        sl = slice(g * POOL_GROUP, (g + 1) * POOL_GROUP)
        w_bd = w_bd.at[sl, sl].set(w_pool[0, g])
    wkv = w_mem_kv[0]
    wo = w_out[0].astype(BF16)
    row = lambda v: v.reshape(1, -1).astype(F32)
    return dict(
        ln_in_g=row(ln_in_g), ln_in_b=row(ln_in_b),
        wql=(wq * lo).astype(BF16), wqh=(wq * hi).astype(BF16), wkt=wk.T.astype(BF16), wv=wv.astype(BF16),
        wp=wp.astype(BF16), wml=(wm * mlo).astype(BF16), wmh=(wm * mhi).astype(BF16),
        mem_wkt=wkv[:, :MEM_WIDTH].T.astype(BF16), mem_wv=wkv[:, MEM_WIDTH:].astype(BF16),
        na_bias=_na_bias_table(na_rpb[0]),
        w_bd=w_bd.astype(BF16), pool_scale=row(pool_scale[0]),
        wo_na=wo[:NA_WIDTH], wo_pool=wo[NA_WIDTH:NA_WIDTH + POOL_WIDTH], wo_mem=wo[NA_WIDTH + POOL_WIDTH:],
        ln1_g=row(ln1_g[0]), ln1_b=row(ln1_b[0]),
        wqt=w_query[0].T.astype(BF16), keys=sub_keys[0].astype(BF16),
        u=_pack_row_pairs(peer_u[0].astype(BF16)), vt=_pack_row_pairs(peer_v[0].T.astype(BF16)),
        ln2_g=row(ln2_g[0]), ln2_b=row(ln2_b[0]),
    )


def _trunk(x, mem, w):
    batch, seq, _ = x.shape
    rows = seq // GRID_W
    xf = x.reshape(batch * seq, D_MODEL)
    xn, ql, qh, kt, v, xp, ml, mh = _in_proj(xf, w["ln_in_g"], w["ln_in_b"], w["wql"], w["wqh"], w["wkt"],
                                             w["wv"], w["wp"], w["wml"], w["wmh"])
    y_na = _na_attention(ql, qh, kt, v, w["na_bias"], batch, rows)
    y_pool = _pool(xp, w["w_bd"], w["pool_scale"], batch, seq)
    mkt, mv = _mem_kv(mem, w["mem_wkt"], w["mem_wv"])
    y_mem = _mem_attention(ml, mh, mkt, mv, batch, seq)
    x1 = _out_proj(xn, y_na, y_pool, y_mem, w["wo_na"], w["wo_pool"], w["wo_mem"], w["ln1_g"], w["ln1_b"])
    s1, e1, phi, a = _route(x1, w["wqt"], w["keys"])
    y = _peer(x1, w["u"], w["vt"], s1, e1, phi, a, w["ln2_g"], w["ln2_b"])
    return y.reshape(batch, seq, D_MODEL)


def kernel(x_prompt, x_sample, mem_prompt, mem_sample, ln_in_g, ln_in_b, w_in, w_mem_kv, na_rpb, w_pool, pool_scale, w_out, ln1_g, ln1_b, w_query, sub_keys, peer_u, peer_v, ln2_g, ln2_b):
    w = _prepare(ln_in_g, ln_in_b, w_in, w_mem_kv, na_rpb, w_pool, pool_scale, w_out, ln1_g, ln1_b,
                 w_query, sub_keys, peer_u, peer_v, ln2_g, ln2_b)
    return (_trunk(x_prompt, mem_prompt, w), _trunk(x_sample, mem_sample, w))
```
